```python
import jax, jax.numpy as jnp
from jax import lax
import numpy as np

D_MODEL = 1024
BATCH = 32
SEQ = 256
DEPTH = 2
DEC_BATCH = 4
DEC_SEQ = 2048
PAST_LEN = 256

GRID_W = 64
D_MIX = D_MODEL
HEAD_A = 64
D_A = D_MIX // 2
H_A = D_A // HEAD_A
D_B = D_MIX // 4
H_B = 4
BLK_B = D_B // H_B
D_C = D_MIX - D_A - D_B
HEAD_C = 64
H_C = D_C // HEAD_C
LORA_W = 64
LORA_A = 64
LORA_G = 128
CONV_B = 4
CONV_F = 3
RG_C = 8.0
CHUNK_C = 64
D_FF = ((8 * D_MODEL // 3 + 255) // 256) * 256
N_DIR = 2
N_MOD = 6
RMS_EPS = 1e-6
GN_EPS = 64e-5
SPLIT_SIZES = (D_A, D_A, D_A, LORA_W, LORA_W, LORA_A, LORA_A, LORA_G,
               D_B, D_B,
               D_C, D_C, D_C, D_C, D_C)
IN_COLS = sum(SPLIT_SIZES)

kernel_name = 'hybrid_prefix_diffusion_step'


def rms_norm(x, g):
    xf = x.astype(jnp.float32)
    y = xf * lax.rsqrt(jnp.mean(xf * xf, axis=-1, keepdims=True) + RMS_EPS)
    return (y * g.astype(jnp.float32)).astype(x.dtype)


def dwconv(x, w, b, pad):
    y = lax.conv_general_dilated(x, w[:, None, :].astype(x.dtype), window_strides=(1,), padding=[pad],
                                 dimension_numbers=('NWC', 'WIO', 'NWC'), feature_group_count=x.shape[-1])
    return y + b.astype(x.dtype)


def modulation(cond, ada_w, ada_b):
    m = jax.nn.silu(cond) @ ada_w + ada_b
    return jnp.split(m[:, None, :], N_MOD, axis=-1)


def rwkv7_recurrence(r, w, k, v, kk, a, s0, reverse):
    def step(S, inp):
        r_t, w_t, k_t, v_t, kk_t, a_t = inp
        s_kk = jnp.einsum('bhvk,bhk->bhv', S, kk_t)
        S = (S * w_t[:, :, None, :] - s_kk[..., None] * (kk_t * a_t)[:, :, None, :]
             + v_t[..., None] * k_t[:, :, None, :])
        return S, jnp.einsum('bhvk,bhk->bhv', S, r_t)
    xs = tuple(jnp.moveaxis(t, 1, 0) for t in (r, w, k, v, kk, a))
    s_fin, o = lax.scan(step, s0, xs, reverse=reverse)
    return jnp.moveaxis(o, 0, 1), s_fin


def rwkv7_mixer(r, k, v, wd, ad, gd, s0, p):
    Bn, T, _ = r.shape
    heads = lambda t: t.reshape(Bn, T, H_A, HEAD_A)
    kk = heads(k * p['rwkv_k_k'])
    kk = kk * lax.rsqrt(jnp.sum(kk * kk, axis=-1, keepdims=True) + 1e-12)
    rh, vh = heads(r), heads(v)
    outs, finals = [], []
    for d in range(N_DIR):
        wl = p['rwkv_w0'][d] + jnp.tanh(wd[d]) @ p['rwkv_w_up'][d]
        decay = jnp.exp(-jnp.exp(-jax.nn.softplus(-wl) - 0.5))
        a = jax.nn.sigmoid(p['rwkv_a0'][d] + ad[d] @ p['rwkv_a_up'][d])
        kd = k * (1.0 + (a - 1.0) * p['rwkv_k_a'])
        od, sd = rwkv7_recurrence(rh, heads(decay), heads(kd), vh, kk, heads(a), s0[:, d], reverse=(d == 1))
        outs.append(od)
        finals.append(sd)
    o = outs[0] + outs[1]
    mu = jnp.mean(o, axis=-1, keepdims=True)
    var = jnp.mean(jnp.square(o - mu), axis=-1, keepdims=True)
    o = ((o - mu) * lax.rsqrt(var + GN_EPS)).reshape(Bn, T, D_A) * p['rwkv_ln_w'] + p['rwkv_ln_b']
    bonus = jnp.sum(rh * heads(k) * p['rwkv_r_k'], axis=-1, keepdims=True) * vh
    o = (o + bonus.reshape(Bn, T, D_A)) * (jax.nn.sigmoid(gd) @ p['rwkv_g_up'])
    return o, jnp.stack(finals, axis=1)


def diag_linear_scan(a, b, h0, reverse):
    def combine(e, l):
        return (e[0] * l[0], l[0] * e[1] + l[1])
    a_cum, b_cum = lax.associative_scan(combine, (a, b), axis=1, reverse=reverse)
    h = a_cum * h0[:, None, :] + b_cum
    return h, (h[:, 0] if reverse else h[:, -1])


def rglru_mixer(xb, gb, h0, p):
    Bn, T, _ = xb.shape
    u = dwconv(xb, p['lru_conv_w'], p['lru_conv_b'], (CONV_B // 2, CONV_B - 1 - CONV_B // 2))
    ub = u.reshape(Bn, T, H_B, BLK_B)
    hs, finals = [], []
    for d in range(N_DIR):
        rg = jax.nn.sigmoid(jnp.einsum('bthi,hij->bthj', ub, p['lru_wa'][d]).reshape(Bn, T, D_B) + p['lru_ba'][d])
        ig = jax.nn.sigmoid(jnp.einsum('bthi,hij->bthj', ub, p['lru_wx'][d]).reshape(Bn, T, D_B) + p['lru_bx'][d])
        log_a = -RG_C * rg * jax.nn.softplus(-p['lru_lambda'][d])
        h, h_fin = diag_linear_scan(jnp.exp(log_a), jnp.sqrt(-jnp.expm1(2.0 * log_a)) * (ig * u),
                                    h0[:, d], reverse=(d == 1))
        hs.append(h)
        finals.append(h_fin)
    y = (hs[0] + hs[1]) * jax.nn.gelu(gb)
    return y, jnp.stack(finals, axis=1)


def hgrn2_chunk_scan(q, log_f, k, v, s0):
    Bn, T, H, _ = q.shape
    n_chunks = T // CHUNK_C
    chunks = lambda t: jnp.moveaxis(t.reshape(Bn, n_chunks, CHUNK_C, H, t.shape[-1]), 1, 0)
    causal = jnp.tril(jnp.ones((CHUNK_C, CHUNK_C), dtype=bool))[None, :, :, None, None]

    def step(S, inp):
        qc, lfc, kc, vc = inp
        b = jnp.cumsum(lfc, axis=1)
        o_inter = jnp.einsum('blhk,bhkv->blhv', qc * jnp.exp(b), S)
        rel = jnp.exp(jnp.where(causal, b[:, :, None] - b[:, None, :], -jnp.inf))
        att = jnp.einsum('bthk,bshk,btshk->bhts', qc, kc, rel)
        o_intra = jnp.einsum('bhts,bshv->bthv', att, vc)
        b_last = b[:, -1]
        S = (jnp.exp(b_last)[..., None] * S
             + jnp.einsum('bshk,bshv->bhkv', kc * jnp.exp(b_last[:, None] - b), vc))
        return S, o_inter + o_intra

    s_fin, o = lax.scan(step, s0, tuple(chunks(t) for t in (q, log_f, k, v)))
    return jnp.moveaxis(o, 0, 1).reshape(Bn, T, H, v.shape[-1]), s_fin


def hgrn2_mixer(q, f_raw, iv, og, s0, p):
    Bn, T, _ = q.shape
    heads = lambda t: t.reshape(Bn, T, H_C, HEAD_C)
    qh, vh = heads(jax.nn.silu(q)), heads(iv)
    lb = p['hgrn_lb']
    outs, finals = [], []
    for d in range(N_DIR):
        f = lb[d] + (1.0 - lb[d]) * jax.nn.sigmoid(f_raw[d])
        args = (qh, heads(jnp.log(f)), heads(1.0 - f), vh)
        if d == 1:
            args = tuple(jnp.flip(t, axis=1) for t in args)
        od, sd = hgrn2_chunk_scan(*args, s0[:, d])
        outs.append(jnp.flip(od, axis=1) if d == 1 else od)
        finals.append(sd)
    o = outs[0] + outs[1]
    o = o * lax.rsqrt(jnp.mean(o * o, axis=-1, keepdims=True) + RMS_EPS)
    o = o.reshape(Bn, T, D_C) * p['hgrn_norm_g'] * jax.nn.silu(og)
    return o, jnp.stack(finals, axis=1)


def conv_ffn(h, p, on_grid):
    g = h @ p['ffn_w_gate']
    Bn, T, F = g.shape
    if on_grid:
        rows = T // GRID_W
        g = dwconv(g.reshape(Bn * rows, GRID_W, F), p['ffn_conv_w'], p['ffn_conv_b'], (1, 1)).reshape(Bn, T, F)
    else:
        g = dwconv(g, p['ffn_conv_w'], p['ffn_conv_b'], (1, 1))
    return (jax.nn.silu(g) * (h @ p['ffn_w_up'])) @ p['ffn_w_down']


def trunk_layer(x, mods, s_rwkv, s_lru, s_hgrn, p, on_grid):
    shift1, scale1, gate1, shift2, scale2, gate2 = mods
    h = rms_norm(x, p['norm_mix_g']) * (1.0 + scale1) + shift1
    idx = np.cumsum(SPLIT_SIZES)[:-1].tolist()
    (r, k, v, wd_f, wd_b, ad_f, ad_b, gd, xb, gb, q, ff, fb, iv, og) = jnp.split(
        (h @ p['w_in']).astype(jnp.float32), idx, axis=-1)
    o_a, s_a = rwkv7_mixer(r, k, v, (wd_f, wd_b), (ad_f, ad_b), gd, s_rwkv, p)
    o_b, s_b = rglru_mixer(xb, gb, s_lru, p)
    o_c, s_c = hgrn2_mixer(q, (ff, fb), iv, og, s_hgrn, p)
    mix = jnp.concatenate([o_a, o_b, o_c], axis=-1).astype(x.dtype) @ p['w_out']
    x = x + gate1 * mix
    h = rms_norm(x, p['norm_ffn_g']) * (1.0 + scale2) + shift2
    x = x + gate2 * conv_ffn(h, p, on_grid)
    return x, s_a, s_b, s_c


def setup_inputs(seed: int = 0) -> dict:
    key = jax.random.key(seed)
    ks = iter(jax.random.split(key, 64))
    nrm = lambda shape, scale: scale * jax.random.normal(next(ks), shape, jnp.float32)
    unif = lambda shape, lo, hi: jax.random.uniform(next(ks), shape, jnp.float32, lo, hi)
    L, D = DEPTH, D_MODEL
    return {
        'x_prompt': nrm((BATCH, SEQ, D), 1.0),
        'x_sample': nrm((DEC_BATCH, DEC_SEQ, D), 1.0),
        'state_rwkv': nrm((DEC_BATCH, DEPTH, N_DIR, H_A, HEAD_A, HEAD_A), 0.5),
        'state_rglru': nrm((DEC_BATCH, DEPTH, N_DIR, D_B), 0.5),
        'state_hgrn': nrm((DEC_BATCH, DEPTH, N_DIR, H_C, HEAD_C, HEAD_C), 0.5),
        'c': nrm((DEC_BATCH, D), 1.0),
        'c_ctx': nrm((D,), 1.0),
        'ada_w': nrm((L, D, N_MOD * D), 0.5 * D ** -0.5),
        'ada_b': nrm((L, N_MOD * D), 0.02),
        'norm_mix_g': 1.0 + nrm((L, D), 0.05),
        'norm_ffn_g': 1.0 + nrm((L, D), 0.05),
        'w_in': nrm((L, D, IN_COLS), D ** -0.5),
        'w_out': nrm((L, D_MIX, D), D_MIX ** -0.5),
        'rwkv_w0': unif((L, N_DIR, D_A), -6.0, 0.0),
        'rwkv_w_up': nrm((L, N_DIR, LORA_W, D_A), 0.1 * LORA_W ** -0.5),
        'rwkv_a0': nrm((L, N_DIR, D_A), 0.1),
        'rwkv_a_up': nrm((L, N_DIR, LORA_A, D_A), 0.1 * LORA_A ** -0.5),
        'rwkv_g_up': nrm((L, LORA_G, D_A), LORA_G ** -0.5),
        'rwkv_k_k': 0.85 + nrm((L, D_A), 0.05),
        'rwkv_k_a': 1.0 + nrm((L, D_A), 0.05),
        'rwkv_r_k': nrm((L, H_A, HEAD_A), 0.1),
        'rwkv_ln_w': 1.0 + nrm((L, D_A), 0.05),
        'rwkv_ln_b': nrm((L, D_A), 0.02),
        'lru_conv_w': nrm((L, CONV_B, D_B), CONV_B ** -0.5),
        'lru_conv_b': nrm((L, D_B), 0.02),
        'lru_wa': nrm((L, N_DIR, H_B, BLK_B, BLK_B), BLK_B ** -0.5),
        'lru_ba': nrm((L, N_DIR, D_B), 0.02),
        'lru_wx': nrm((L, N_DIR, H_B, BLK_B, BLK_B), BLK_B ** -0.5),
        'lru_bx': nrm((L, N_DIR, D_B), 0.02),
        'lru_lambda': unif((L, N_DIR, D_B), 4.3, 9.0),
        'hgrn_lb_logits': nrm((N_DIR, L, D_C), 1.0),
        'hgrn_norm_g': 1.0 + nrm((L, D_C), 0.05),
        'ffn_w_gate': nrm((L, D, D_FF), D ** -0.5),
        'ffn_w_up': nrm((L, D, D_FF), D ** -0.5),
        'ffn_conv_w': nrm((L, CONV_F, D_FF), CONV_F ** -0.5),
        'ffn_conv_b': nrm((L, D_FF), 0.02),
        'ffn_w_down': nrm((L, D_FF, D), D_FF ** -0.5),
        'final_g': 1.0 + nrm((D,), 0.05),
    }


def reference(x_prompt, x_sample, state_rwkv, state_rglru, state_hgrn, c, c_ctx, ada_w, ada_b,
              norm_mix_g, norm_ffn_g, w_in, w_out, rwkv_w0, rwkv_w_up, rwkv_a0, rwkv_a_up, rwkv_g_up,
              rwkv_k_k, rwkv_k_a, rwkv_r_k, rwkv_ln_w, rwkv_ln_b, lru_conv_w, lru_conv_b, lru_wa, lru_ba,
              lru_wx, lru_bx, lru_lambda, hgrn_lb_logits, hgrn_norm_g, ffn_w_gate, ffn_w_up, ffn_conv_w,
              ffn_conv_b, ffn_w_down, final_g):
    f32 = jnp.float32
    lb_all = jnp.cumsum(jax.nn.softmax(hgrn_lb_logits.astype(f32), axis=1), axis=1)
    lb_all = lb_all - lb_all[:, :1]
    n_ctx = x_prompt.shape[0]
    z_rwkv = jnp.zeros((n_ctx, N_DIR, H_A, HEAD_A, HEAD_A), f32)
    z_lru = jnp.zeros((n_ctx, N_DIR, D_B), f32)
    z_hgrn = jnp.zeros((n_ctx, N_DIR, H_C, HEAD_C, HEAD_C), f32)
    xp, xs = x_prompt, x_sample
    new_rwkv, new_lru, new_hgrn = [], [], []
    for l in range(DEPTH):
        p = {
            'norm_mix_g': norm_mix_g[l], 'norm_ffn_g': norm_ffn_g[l], 'w_in': w_in[l], 'w_out': w_out[l],
            'rwkv_w0': rwkv_w0[l], 'rwkv_w_up': rwkv_w_up[l], 'rwkv_a0': rwkv_a0[l], 'rwkv_a_up': rwkv_a_up[l],
            'rwkv_g_up': rwkv_g_up[l], 'rwkv_k_k': rwkv_k_k[l], 'rwkv_k_a': rwkv_k_a[l], 'rwkv_r_k': rwkv_r_k[l],
            'rwkv_ln_w': rwkv_ln_w[l], 'rwkv_ln_b': rwkv_ln_b[l],
            'lru_conv_w': lru_conv_w[l], 'lru_conv_b': lru_conv_b[l], 'lru_wa': lru_wa[l], 'lru_ba': lru_ba[l],
            'lru_wx': lru_wx[l], 'lru_bx': lru_bx[l], 'lru_lambda': lru_lambda[l],
            'hgrn_lb': lb_all[:, l], 'hgrn_norm_g': hgrn_norm_g[l],
            'ffn_w_gate': ffn_w_gate[l], 'ffn_w_up': ffn_w_up[l], 'ffn_conv_w': ffn_conv_w[l],
            'ffn_conv_b': ffn_conv_b[l], 'ffn_w_down': ffn_w_down[l],
        }
        xp, s_a, s_b, s_c = trunk_layer(xp, modulation(c_ctx[None, :], ada_w[l], ada_b[l]),
                                        z_rwkv, z_lru, z_hgrn, p, on_grid=False)
        new_rwkv.append(s_a)
        new_lru.append(s_b)
        new_hgrn.append(s_c)
        xs, _, _, _ = trunk_layer(xs, modulation(c, ada_w[l], ada_b[l]),
                                  state_rwkv[:, l].astype(f32), state_rglru[:, l].astype(f32),
                                  state_hgrn[:, l].astype(f32), p, on_grid=True)
    y_prompt = rms_norm(xp, final_g)
    y_sample = rms_norm(xs, final_g)
    new_rwkv_state = jnp.stack(new_rwkv, axis=1).astype(x_prompt.dtype)
    new_rglru_state = jnp.stack(new_lru, axis=1).astype(x_prompt.dtype)
    new_hgrn_state = jnp.stack(new_hgrn, axis=1).astype(x_prompt.dtype)
    return (y_prompt, y_sample, new_rwkv_state, new_rglru_state, new_hgrn_state)
```

```python
import functools
import math

import jax
import jax.numpy as jnp
from jax import lax
from jax.experimental import pallas as pl
from jax.experimental.pallas import tpu as pltpu

F32 = jnp.float32
BF16 = jnp.bfloat16

D_MODEL = 1024
HEAD = 64
PAIR = 2 * HEAD
D_A = 512
D_B = 256
D_C = 256
LORA_G = 128
N_MOD = 6
D_FF = 2816
FF_TILE = 256
PA_COLS = 3 * D_A + 4 * 64 + LORA_G
PB_COLS = 2 * D_B
PC_COLS = 5 * D_C
CHUNK = 64
BLOCK_T = 256
TOKEN_TILE = 512
RMS_EPS = 1e-6
GN_EPS = 64e-5
RG_C = 8.0
VMEM_LIMIT = 48 * 1024 * 1024


def _dot(a, b):
    return jnp.dot(a.astype(BF16), b.astype(BF16), preferred_element_type=F32)


def _dot_nt(a, b):
    return lax.dot_general(a.astype(BF16), b.astype(BF16), (((1,), (1,)), ((), ())),
                           preferred_element_type=F32)


def _dot_tn(a, b):
    return lax.dot_general(a.astype(BF16), b.astype(BF16), (((0,), (0,)), ((), ())),
                           preferred_element_type=F32)


def _dot_f32(a, b):
    return jnp.dot(a, b, precision=lax.Precision.HIGHEST, preferred_element_type=F32)


def _sigmoid(x):
    return 1.0 / (1.0 + jnp.exp(-x))


def _silu(x):
    return x * _sigmoid(x)


def _softplus(x):
    return jnp.maximum(x, 0.0) + jnp.log1p(jnp.exp(-jnp.abs(x)))


def _iota2(shape, axis):
    return lax.broadcasted_iota(jnp.int32, shape, axis)


def _head_sum_matrix():
    return ((_iota2((PAIR, PAIR), 0) >> 6) == (_iota2((PAIR, PAIR), 1) >> 6)).astype(F32)


def _head_sums(x, ind):
    return jnp.concatenate([_dot_f32(x[:, i:i + PAIR], ind) for i in range(0, x.shape[1], PAIR)], axis=1)


def _cumsum_matrix(rev):
    t, s = _iota2((CHUNK, CHUNK), 0), _iota2((CHUNK, CHUNK), 1)
    return ((s >= t) if rev else (s <= t)).astype(F32)


def _stack_heads(x):
    first = _iota2(x.shape, 1) < HEAD
    return jnp.concatenate([jnp.where(first, x, 0.0), jnp.where(first, 0.0, x)], axis=0)


def _fold_heads(x):
    return x[:CHUNK] + x[CHUNK:]


def _pair_masks(rev):
    i, j = _iota2((PAIR, PAIR), 0), _iota2((PAIR, PAIR), 1)
    same = (i >> 6) == (j >> 6)
    t, s = i & (CHUNK - 1), j & (CHUNK - 1)
    strict = same & ((s > t) if rev else (s < t))
    incl = same & ((s >= t) if rev else (s <= t))
    return same, strict, incl, t, s


def _load_state_pairs(s0_ref, s_scr, n_pairs, transpose):
    z = jnp.zeros((HEAD, HEAD), F32)
    for p in range(n_pairs):
        a, b = s0_ref[2 * p], s0_ref[2 * p + 1]
        if transpose:
            a, b = a.T, b.T
        s_scr[p] = jnp.concatenate([jnp.concatenate([a, z], axis=1), jnp.concatenate([z, b], axis=1)], axis=0)


def _store_state_pairs(sfin_ref, s_scr, n_pairs, transpose):
    for p in range(n_pairs):
        s = s_scr[p]
        a, b = s[:HEAD, :HEAD], s[HEAD:, HEAD:]
        if transpose:
            a, b = a.T, b.T
        sfin_ref[2 * p] = a
        sfin_ref[2 * p + 1] = b


def _mod_kernel(c_ref, w_ref, b_ref, o_ref):
    o_ref[...] = _dot_f32(_silu(c_ref[...]), w_ref[...]) + b_ref[...]


def _modulation(cond, ada_w, ada_b):
    n_layer, _, n_out = ada_w.shape
    tn = 512
    return pl.pallas_call(
        _mod_kernel,
        grid=(n_layer, n_out // tn),
        in_specs=[pl.BlockSpec((8, D_MODEL), lambda l, j: (0, 0)),
                  pl.BlockSpec((None, D_MODEL, tn), lambda l, j: (l, 0, j)),
                  pl.BlockSpec((None, 1, tn), lambda l, j: (l, 0, j))],
        out_specs=pl.BlockSpec((None, 8, tn), lambda l, j: (l, 0, j)),
        out_shape=jax.ShapeDtypeStruct((n_layer, 8, n_out), F32),
        name="modulation",
    )(cond, ada_w, ada_b)


def _rms(x):
    return x * lax.rsqrt(jnp.mean(x * x, axis=-1, keepdims=True) + RMS_EPS)


def _norm_in_kernel(x_ref, mod_ref, g_ref, w_ref, pa_ref, pb_ref, pc_ref):
    h = _rms(x_ref[...]) * g_ref[...]
    h = (h * (1.0 + mod_ref[:, D_MODEL:2 * D_MODEL]) + mod_ref[:, 0:D_MODEL]).astype(BF16)
    pa_ref[...] = jnp.dot(h, w_ref[:, 0:PA_COLS], preferred_element_type=F32)
    pb_ref[...] = jnp.dot(h, w_ref[:, PA_COLS:PA_COLS + PB_COLS], preferred_element_type=F32)
    pc_ref[...] = jnp.dot(h, w_ref[:, PA_COLS + PB_COLS:], preferred_element_type=F32)


def _norm_in(x, mods, g, w_in, seq_len):
    n = x.shape[0]
    tiles_per_seq = seq_len // TOKEN_TILE if mods.shape[0] > 1 else None
    mod_map = (lambda i: (i // tiles_per_seq, 0, 0)) if tiles_per_seq else (lambda i: (0, 0, 0))
    const = lambda i: (0, 0)
    return pl.pallas_call(
        _norm_in_kernel,
        grid=(n // TOKEN_TILE,),
        in_specs=[pl.BlockSpec((TOKEN_TILE, D_MODEL), lambda i: (i, 0)),
                  pl.BlockSpec((None, 1, N_MOD * D_MODEL), mod_map),
                  pl.BlockSpec((1, D_MODEL), const),
                  pl.BlockSpec(w_in.shape, const, pipeline_mode=pl.Buffered(1))],
        out_specs=[pl.BlockSpec((TOKEN_TILE, PA_COLS), lambda i: (i, 0)),
                   pl.BlockSpec((TOKEN_TILE, PB_COLS), lambda i: (i, 0)),
                   pl.BlockSpec((TOKEN_TILE, PC_COLS), lambda i: (i, 0))],
        out_shape=[jax.ShapeDtypeStruct((n, PA_COLS), F32),
                   jax.ShapeDtypeStruct((n, PB_COLS), F32),
                   jax.ShapeDtypeStruct((n, PC_COLS), F32)],
        compiler_params=pltpu.CompilerParams(vmem_limit_bytes=VMEM_LIMIT),
        name="norm_in",
    )(x, mods, g, w_in)


def _rwkv_kernel(*refs, rev, has_init, want_final, post, n_blk):
    refs = list(refs)
    pa_ref = refs.pop(0)
    of_ref = refs.pop(0) if post else None
    s0_ref = refs.pop(0) if has_init else None
    w0_ref, wup_ref, a0_ref, aup_ref, kkw_ref, kaw_ref = (refs.pop(0) for _ in range(6))
    if post:
        gup_ref, rk_ref, lnw_ref, lnb_ref = (refs.pop(0) for _ in range(4))
    o_ref = refs.pop(0)
    sfin_ref = refs.pop(0) if want_final else None
    s_scr, kd_scr, kk_scr, beta_scr, logw_scr = (refs.pop(0) for _ in range(5))
    ob_scr = refs.pop(0) if post else None

    n_pairs = D_A // PAIR
    j = pl.program_id(1)
    ind = _head_sum_matrix()

    @pl.when(j == 0)
    def _():
        if has_init:
            _load_state_pairs(s0_ref, s_scr, n_pairs, transpose=False)
        else:
            s_scr[...] = jnp.zeros(s_scr.shape, F32)

    k = pa_ref[:, D_A:2 * D_A]
    wl = w0_ref[...] + _dot(jnp.tanh(pa_ref[:, 3 * D_A:3 * D_A + PAIR]), wup_ref[...])
    logw_scr[...] = -_sigmoid(wl) * math.exp(-0.5)
    a = _sigmoid(a0_ref[...] + _dot(pa_ref[:, 3 * D_A + PAIR:3 * D_A + 2 * PAIR], aup_ref[...]))
    kd_scr[...] = k * (1.0 + (a - 1.0) * kaw_ref[...])
    kk = k * kkw_ref[...]
    kk = kk * lax.rsqrt(_head_sums(kk * kk, ind) + 1e-12)
    kk_scr[...] = kk
    beta_scr[...] = kk * a

    tri = _cumsum_matrix(rev)
    _, strict, incl, t_idx, s_idx = _pair_masks(rev)
    ident = (_iota2((PAIR, PAIR), 0) == _iota2((PAIR, PAIR), 1)).astype(F32)
    out_buf = ob_scr if post else o_ref

    def chunk_body(c, carry):
        cc = (BLOCK_T // CHUNK - 1 - c) if rev else c
        rows = pl.ds(pl.multiple_of(cc * CHUNK, CHUNK), CHUNK)
        logw = logw_scr[rows, :]
        b = _dot_f32(tri, logw)
        b_last = b[0:1, :] if rev else b[CHUNK - 1:CHUNK, :]
        p_inv = jnp.exp(-b)
        p_end = jnp.exp(b_last - b)
        r_t = pa_ref[rows, 0:D_A] * jnp.exp(b)
        k_t = kk_scr[rows, :] * jnp.exp(b - logw)
        kd = kd_scr[rows, :]
        beta = beta_scr[rows, :]
        k_h, b_h = kd * p_inv, beta * p_inv
        k_e, b_e = kd * p_end, beta * p_end
        v = pa_ref[rows, 2 * D_A:3 * D_A]
        decay_end = jnp.exp(b_last)
        for p in range(n_pairs):
            ln = slice(p * PAIR, (p + 1) * PAIR)
            s_bd = s_scr[p]
            kt_st = _stack_heads(k_t[:, ln])
            lhs = jnp.concatenate([kt_st, _stack_heads(r_t[:, ln])], axis=0)
            sk = _dot_nt(lhs, jnp.concatenate([k_h[:, ln], k_h[:, ln]], axis=0))
            sb = _dot_nt(lhs, jnp.concatenate([b_h[:, ln], b_h[:, ln]], axis=0))
            a_kk = jnp.where(strict, sk[:PAIR], 0.0)
            a_rk = jnp.where(incl, sk[PAIR:], 0.0)
            a_kb = jnp.where(strict, sb[:PAIR], 0.0)
            a_rb = jnp.where(incl, sb[PAIR:], 0.0)
            pw = -a_kb
            t_inv = ident + pw
            for _ in range(5):
                pw = _dot_f32(pw, pw)
                t_inv = t_inv + _dot_f32(t_inv, pw)
            ys = _dot_nt(jnp.concatenate([kt_st, r_t[:, ln]], axis=0), s_bd)
            v_st = _stack_heads(v[:, ln])
            u = _dot_f32(t_inv, ys[:PAIR] + _dot(a_kk, v_st))
            vu = jnp.concatenate([v_st, u], axis=0)
            o2 = _dot(jnp.concatenate([a_rk, -a_rb], axis=1), vu)
            out_buf[rows, ln] = ys[PAIR:] + _fold_heads(o2)
            kb_e = jnp.concatenate([_stack_heads(k_e[:, ln]), -_stack_heads(b_e[:, ln])], axis=0)
            s_scr[p] = s_bd * decay_end[:, ln] + _dot_tn(vu, kb_e)
        return carry

    lax.fori_loop(0, BLOCK_T // CHUNK, chunk_body, 0)

    if post:
        r = pa_ref[:, 0:D_A]
        o = of_ref[...] + ob_scr[...]
        mu = _head_sums(o, ind) * (1.0 / HEAD)
        d = o - mu
        var = _head_sums(d * d, ind) * (1.0 / HEAD)
        o = d * lax.rsqrt(var + GN_EPS) * lnw_ref[...] + lnb_ref[...]
        bonus = _head_sums(r * pa_ref[:, D_A:2 * D_A] * rk_ref[...], ind) * pa_ref[:, 2 * D_A:3 * D_A]
        gate = _dot(_sigmoid(pa_ref[:, 3 * D_A + 2 * PAIR:3 * D_A + 2 * PAIR + LORA_G]), gup_ref[...])
        o_ref[...] = ((o + bonus) * gate).astype(o_ref.dtype)

    if want_final:
        @pl.when(j == n_blk - 1)
        def _():
            _store_state_pairs(sfin_ref, s_scr, n_pairs, transpose=False)


def _blk_map(n_blk, rev):
    if rev:
        return lambda b, j: (b * n_blk + n_blk - 1 - j, 0)
    return lambda b, j: (b * n_blk + j, 0)


def _row(v):
    return v.reshape(1, -1)


def _rwkv_dir(pa, o_fwd, s0, wts, *, batch, seq_len, rev, want_final):
    n = pa.shape[0]
    n_blk = seq_len // BLOCK_T
    post = rev
    has_init = s0 is not None
    blk = _blk_map(n_blk, rev)
    const = lambda b, j: (0, 0)
    state_spec = pl.BlockSpec((None, D_A // HEAD, HEAD, HEAD), lambda b, j: (b, 0, 0, 0))
    d = 1 if rev else 0
    ins, specs = [pa], [pl.BlockSpec((BLOCK_T, PA_COLS), blk)]
    if post:
        ins.append(o_fwd)
        specs.append(pl.BlockSpec((BLOCK_T, D_A), blk))
    if has_init:
        ins.append(s0)
        specs.append(state_spec)
    small = [_row(wts['w0'][d]), wts['w_up_pad'][d], _row(wts['a0'][d]), wts['a_up_pad'][d],
             _row(wts['k_k']), _row(wts['k_a'])]
    if post:
        small += [wts['g_up'], _row(wts['r_k']), _row(wts['ln_w']), _row(wts['ln_b'])]
    for wgt in small:
        ins.append(wgt)
        specs.append(pl.BlockSpec(wgt.shape, const))
    out_shape = [jax.ShapeDtypeStruct((n, D_A), BF16 if post else F32)]
    out_specs = [pl.BlockSpec((BLOCK_T, D_A), blk)]
    if want_final:
        out_shape.append(jax.ShapeDtypeStruct((batch, D_A // HEAD, HEAD, HEAD), F32))
        out_specs.append(state_spec)
    scratch = [pltpu.VMEM((D_A // PAIR, PAIR, PAIR), F32)] + [pltpu.VMEM((BLOCK_T, D_A), F32)] * 4
    if post:
        scratch.append(pltpu.VMEM((BLOCK_T, D_A), F32))
    res = pl.pallas_call(
        functools.partial(_rwkv_kernel, rev=rev, has_init=has_init, want_final=want_final, post=post,
                          n_blk=n_blk),
        grid=(batch, n_blk),
        in_specs=specs, out_specs=out_specs, out_shape=out_shape, scratch_shapes=scratch,
        compiler_params=pltpu.CompilerParams(vmem_limit_bytes=VMEM_LIMIT),
        name="rwkv_bwd" if rev else "rwkv_fwd",
    )(*ins)
    return res[0], (res[1] if want_final else None)


def _block_edges(b, rev):
    t = _iota2(b.shape, 0)
    e = b
    out = {}
    for kbit in (1, 2, 4, 8, 16):
        if rev:
            e = jnp.where((t & kbit) != 0, pltpu.roll(e, kbit, 0), e)
        else:
            e = jnp.where((t & kbit) == 0, pltpu.roll(e, CHUNK - kbit, 0), e)
        if kbit >= 4:
            out[2 * kbit] = e
    return out


def _hgrn_kernel(*refs, rev, has_init, want_final, post, n_blk):
    refs = list(refs)
    pc_ref = refs.pop(0)
    of_ref = refs.pop(0) if post else None
    s0_ref = refs.pop(0) if has_init else None
    lb_ref = refs.pop(0)
    ng_ref = refs.pop(0) if post else None
    o_ref = refs.pop(0)
    sfin_ref = refs.pop(0) if want_final else None
    s_scr, q_scr, logf_scr, kf_scr = (refs.pop(0) for _ in range(4))
    ob_scr = refs.pop(0) if post else None

    n_pairs = D_C // PAIR
    j = pl.program_id(1)
    ind = _head_sum_matrix()

    @pl.when(j == 0)
    def _():
        if has_init:
            _load_state_pairs(s0_ref, s_scr, n_pairs, transpose=True)
        else:
            s_scr[...] = jnp.zeros(s_scr.shape, F32)

    q_scr[...] = _silu(pc_ref[:, 0:D_C])
    f_col = 2 * D_C if rev else D_C
    lb = lb_ref[...]
    f = lb + (1.0 - lb) * _sigmoid(pc_ref[:, f_col:f_col + D_C])
    logf_scr[...] = jnp.log(f)
    kf_scr[...] = 1.0 - f

    tri = _cumsum_matrix(rev)
    same, _, _, t_idx, s_idx = _pair_masks(rev)
    level_mask = {}
    for h in (8, 16, 32):
        sh = h.bit_length() - 1
        step = -1 if rev else 1
        level_mask[h] = same & ((t_idx >> sh) == (s_idx >> sh) + step) & ((t_idx >> (sh + 1)) == (s_idx >> (sh + 1)))
    t_row = _iota2((CHUNK, PAIR), 0)
    out_buf = ob_scr if post else o_ref

    def chunk_body(c, carry):
        cc = (BLOCK_T // CHUNK - 1 - c) if rev else c
        rows = pl.ds(pl.multiple_of(cc * CHUNK, CHUNK), CHUNK)
        b_all = _dot_f32(tri, logf_scr[rows, :])
        for p in range(n_pairs):
            ln = slice(p * PAIR, (p + 1) * PAIR)
            b = b_all[:, ln]
            b_last = b[0:1, :] if rev else b[CHUNK - 1:CHUNK, :]
            q = q_scr[rows, ln]
            kf = kf_scr[rows, ln]
            v = pc_ref[rows, 3 * D_C + p * PAIR:3 * D_C + (p + 1) * PAIR]
            s_bd = s_scr[p]
            o = _dot_nt(q * jnp.exp(b), s_bd)
            edges = _block_edges(b, rev)
            score = jnp.zeros((PAIR, PAIR), F32)
            for h in (8, 16, 32):
                own = edges[h]
                if rev:
                    partner = jnp.concatenate([own[h:], own[CHUNK - h:]], axis=0)
                else:
                    partner = jnp.concatenate([own[:h], own[:CHUNK - h]], axis=0)
                kp = kf * jnp.exp(jnp.minimum(own - b, 0.0))
                qp = q * jnp.exp(jnp.minimum(b - partner, 0.0))
                sc = _dot_nt(_stack_heads(qp), jnp.concatenate([kp, kp], axis=0))
                score = jnp.where(level_mask[h], sc, score)
            v_st = _stack_heads(v)
            o = o + _fold_heads(_dot(score, v_st))
            for delta in range(8):
                if delta == 0:
                    x, v_s = q * kf, v
                else:
                    sh = (CHUNK - delta) if rev else delta
                    b_s, k_s, v_s = (pltpu.roll(z, sh, 0) for z in (b, kf, v))
                    x = q * jnp.exp(jnp.minimum(b - b_s, 0.0)) * k_s
                    ok = ((t_row & 7) <= 7 - delta) if rev else ((t_row & 7) >= delta)
                    x = jnp.where(ok, x, 0.0)
                o = o + _dot(x, ind) * v_s
            out_buf[rows, ln] = o
            k_e = kf * jnp.exp(b_last - b)
            s_scr[p] = s_bd * jnp.exp(b_last) + _dot_tn(v_st, _stack_heads(k_e))
        return carry

    lax.fori_loop(0, BLOCK_T // CHUNK, chunk_body, 0)

    if post:
        o = of_ref[...] + ob_scr[...]
        ms = _head_sums(o * o, ind) * (1.0 / HEAD)
        o = o * lax.rsqrt(ms + RMS_EPS) * ng_ref[...] * _silu(pc_ref[:, 4 * D_C:5 * D_C])
        o_ref[...] = o.astype(o_ref.dtype)

    if want_final:
        @pl.when(j == n_blk - 1)
        def _():
            _store_state_pairs(sfin_ref, s_scr, n_pairs, transpose=True)


def _hgrn_dir(pc, o_fwd, s0, lb, norm_g, *, batch, seq_len, rev, want_final):
    n = pc.shape[0]
    n_blk = seq_len // BLOCK_T
    post = rev
    has_init = s0 is not None
    blk = _blk_map(n_blk, rev)
    const = lambda b, j: (0, 0)
    state_spec = pl.BlockSpec((None, D_C // HEAD, HEAD, HEAD), lambda b, j: (b, 0, 0, 0))
    ins, specs = [pc], [pl.BlockSpec((BLOCK_T, PC_COLS), blk)]
    if post:
        ins.append(o_fwd)
        specs.append(pl.BlockSpec((BLOCK_T, D_C), blk))
    if has_init:
        ins.append(s0)
        specs.append(state_spec)
    ins.append(_row(lb))
    specs.append(pl.BlockSpec((1, D_C), const))
    if post:
        ins.append(_row(norm_g))
        specs.append(pl.BlockSpec((1, D_C), const))
    out_shape = [jax.ShapeDtypeStruct((n, D_C), BF16 if post else F32)]
    out_specs = [pl.BlockSpec((BLOCK_T, D_C), blk)]
    if want_final:
        out_shape.append(jax.ShapeDtypeStruct((batch, D_C // HEAD, HEAD, HEAD), F32))
        out_specs.append(state_spec)
    scratch = [pltpu.VMEM((D_C // PAIR, PAIR, PAIR), F32)] + [pltpu.VMEM((BLOCK_T, D_C), F32)] * 3
    if post:
        scratch.append(pltpu.VMEM((BLOCK_T, D_C), F32))
    res = pl.pallas_call(
        functools.partial(_hgrn_kernel, rev=rev, has_init=has_init, want_final=want_final, post=post,
                          n_blk=n_blk),
        grid=(batch, n_blk),
        in_specs=specs, out_specs=out_specs, out_shape=out_shape, scratch_shapes=scratch,
        compiler_params=pltpu.CompilerParams(vmem_limit_bytes=VMEM_LIMIT),
        name="hgrn_bwd" if rev else "hgrn_fwd",
    )(*ins)
    return res[0], (res[1] if want_final else None)


def _shift_rows(x, s, t_idx):
    n = x.shape[0]
    y = pltpu.roll(x, s % n, 0)
    return jnp.where((t_idx >= s) if s > 0 else (t_idx < n + s), y, 0.0)


def _linear_scan(a, b, t_idx, rev):
    n = a.shape[0]
    k = 1
    while k < n:
        ok = (t_idx < n - k) if rev else (t_idx >= k)
        sh = (n - k) if rev else k
        b = jnp.where(ok, a * pltpu.roll(b, sh, 0) + b, b)
        a = jnp.where(ok, a * pltpu.roll(a, sh, 0), a)
        k *= 2
    return a, b


def _gelu_tanh(x):
    return 0.5 * x * (1.0 + jnp.tanh(math.sqrt(2.0 / math.pi) * (x + 0.044715 * x * x * x)))


def _lru_kernel(*refs, has_init, want_final):
    refs = list(refs)
    pb_ref = refs.pop(0)
    h0_ref = refs.pop(0) if has_init else None
    cw_ref, cb_ref, wa_ref, ba_ref, wx_ref, bx_ref, lam_ref = (refs.pop(0) for _ in range(7))
    y_ref = refs.pop(0)
    hfin_ref = refs.pop(0) if want_final else None

    n = pb_ref.shape[0]
    t_idx = _iota2((n, D_B), 0)
    xb = pb_ref[:, 0:D_B]
    u = cb_ref[...] + cw_ref[2:3, :] * xb
    u = u + cw_ref[0:1, :] * _shift_rows(xb, 2, t_idx)
    u = u + cw_ref[1:2, :] * _shift_rows(xb, 1, t_idx)
    u = u + cw_ref[3:4, :] * _shift_rows(xb, -1, t_idx)
    h_sum = jnp.zeros((n, D_B), F32)
    for d in range(2):
        rev = d == 1
        rg = _sigmoid(_dot(u, wa_ref[d]) + ba_ref[d:d + 1, :])
        ig = _sigmoid(_dot(u, wx_ref[d]) + bx_ref[d:d + 1, :])
        log_a = -RG_C * rg * _softplus(-lam_ref[d:d + 1, :])
        a_t = jnp.exp(log_a)
        in_scale = jnp.sqrt((1.0 + a_t * a_t) * jnp.tanh(-log_a))
        a_cum, h = _linear_scan(a_t, in_scale * (ig * u), t_idx, rev)
        if has_init:
            h = a_cum * h0_ref[d:d + 1, :] + h
        if want_final:
            hfin_ref[d:d + 1, :] = h[0:1, :] if rev else h[n - 1:n, :]
        h_sum = h_sum + h
    y_ref[...] = (h_sum * _gelu_tanh(pb_ref[:, D_B:2 * D_B])).astype(y_ref.dtype)


def _lru(pb, h0, wts, *, batch, seq_len, want_final):
    n = pb.shape[0]
    has_init = h0 is not None
    const2 = lambda b: (0, 0)
    const3 = lambda b: (0, 0, 0)
    ins, specs = [pb], [pl.BlockSpec((seq_len, PB_COLS), lambda b: (b, 0))]
    state_spec = pl.BlockSpec((None, 2, D_B), lambda b: (b, 0, 0))
    if has_init:
        ins.append(h0)
        specs.append(state_spec)
    for wgt in (wts['conv_w'], _row(wts['conv_b']), wts['wa_bd'], wts['ba'], wts['wx_bd'], wts['bx'],
                wts['lam']):
        ins.append(wgt)
        specs.append(pl.BlockSpec(wgt.shape, const3 if wgt.ndim == 3 else const2))
    out_shape = [jax.ShapeDtypeStruct((n, D_B), BF16)]
    out_specs = [pl.BlockSpec((seq_len, D_B), lambda b: (b, 0))]
    if want_final:
        out_shape.append(jax.ShapeDtypeStruct((batch, 2, D_B), F32))
        out_specs.append(state_spec)
    res = pl.pallas_call(
        functools.partial(_lru_kernel, has_init=has_init, want_final=want_final),
        grid=(batch,),
        in_specs=specs, out_specs=out_specs, out_shape=out_shape,
        compiler_params=pltpu.CompilerParams(vmem_limit_bytes=VMEM_LIMIT),
        name="lru",
    )(*ins)
    return res[0], (res[1] if want_final else None)


def _out_ffn_kernel(*refs, conv_period, final):
    refs = list(refs)
    x_ref, ma_ref, mb_ref, mc_ref, mod_ref, g_ref, wo_ref, wg_ref, wu_ref, wd_ref, cw_ref, cb_ref = (
        refs.pop(0) for _ in range(12))
    fg_ref = refs.pop(0) if final else None
    xo_ref = refs.pop(0)
    yo_ref = refs.pop(0) if final else None

    dm = D_MODEL
    mix = jnp.dot(ma_ref[...], wo_ref[0:D_A, :], preferred_element_type=F32)
    mix = mix + jnp.dot(mb_ref[...], wo_ref[D_A:D_A + D_B, :], preferred_element_type=F32)
    mix = mix + jnp.dot(mc_ref[...], wo_ref[D_A + D_B:, :], preferred_element_type=F32)
    x = x_ref[...] + mod_ref[:, 2 * dm:3 * dm] * mix
    h = _rms(x) * g_ref[...]
    h = (h * (1.0 + mod_ref[:, 4 * dm:5 * dm]) + mod_ref[:, 3 * dm:4 * dm]).astype(BF16)
    tm = x.shape[0]
    t_idx = _iota2((tm, FF_TILE), 0) & (conv_period - 1)
    acc = jnp.zeros((tm, dm), F32)
    for f0 in range(0, D_FF, FF_TILE):
        cols = slice(f0, f0 + FF_TILE)
        g = jnp.dot(h, wg_ref[:, cols], preferred_element_type=F32)
        g_prev = jnp.where(t_idx == 0, 0.0, pltpu.roll(g, 1, 0))
        g_next = jnp.where(t_idx == conv_period - 1, 0.0, pltpu.roll(g, tm - 1, 0))
        g = cw_ref[0:1, cols] * g_prev + cw_ref[1:2, cols] * g + cw_ref[2:3, cols] * g_next + cb_ref[:, cols]
        up = jnp.dot(h, wu_ref[:, cols], preferred_element_type=F32)
        acc = acc + jnp.dot((_silu(g) * up).astype(BF16), wd_ref[cols, :], preferred_element_type=F32)
    x = x + mod_ref[:, 5 * dm:6 * dm] * acc
    xo_ref[...] = x
    if final:
        yo_ref[...] = _rms(x) * fg_ref[...]


def _out_ffn(x, mix_a, mix_b, mix_c, mods, wts, *, seq_len, conv_period, final_g):
    n = x.shape[0]
    final = final_g is not None
    tiles_per_seq = seq_len // TOKEN_TILE if mods.shape[0] > 1 else None
    mod_map = (lambda i: (i // tiles_per_seq, 0, 0)) if tiles_per_seq else (lambda i: (0, 0, 0))
    const = lambda i: (0, 0)
    tok = lambda width: pl.BlockSpec((TOKEN_TILE, width), lambda i: (i, 0))
    resident = lambda w: pl.BlockSpec(w.shape, const, pipeline_mode=pl.Buffered(1))
    ins = [x, mix_a, mix_b, mix_c, mods, _row(wts['norm_g']), wts['w_out'], wts['w_gate'], wts['w_up'],
           wts['w_down'], wts['conv_w'], _row(wts['conv_b'])]
    specs = [tok(D_MODEL), tok(D_A), tok(D_B), tok(D_C), pl.BlockSpec((None, 1, N_MOD * D_MODEL), mod_map),
             pl.BlockSpec((1, D_MODEL), const), resident(wts['w_out']), resident(wts['w_gate']),
             resident(wts['w_up']), resident(wts['w_down']), pl.BlockSpec(wts['conv_w'].shape, const),
             pl.BlockSpec((1, D_FF), const)]
    out_shape = [jax.ShapeDtypeStruct((n, D_MODEL), F32)]
    out_specs = [tok(D_MODEL)]
    if final:
        ins.append(_row(final_g))
        specs.append(pl.BlockSpec((1, D_MODEL), const))
        out_shape.append(jax.ShapeDtypeStruct((n, D_MODEL), F32))
        out_specs.append(tok(D_MODEL))
    res = pl.pallas_call(
        functools.partial(_out_ffn_kernel, conv_period=conv_period, final=final),
        grid=(n // TOKEN_TILE,),
        in_specs=specs, out_specs=out_specs, out_shape=out_shape,
        compiler_params=pltpu.CompilerParams(vmem_limit_bytes=VMEM_LIMIT),
        name="out_ffn",
    )(*ins)
    return res[0], (res[1] if final else None)


def _trunk_layer(x, mods, s_rwkv, s_lru, s_hgrn, wts, *, batch, seq_len, conv_period, want_final, final_g):
    pa, pb, pc = _norm_in(x, mods, _row(wts['norm_mix_g']), wts['w_in'], seq_len)
    kw = dict(batch=batch, seq_len=seq_len, want_final=want_final)
    sel = lambda s, d: None if s is None else s[:, d]
    o_f, sa_f = _rwkv_dir(pa, None, sel(s_rwkv, 0), wts['rwkv'], rev=False, **kw)
    mix_a, sa_b = _rwkv_dir(pa, o_f, sel(s_rwkv, 1), wts['rwkv'], rev=True, **kw)
    c_f, sc_f = _hgrn_dir(pc, None, sel(s_hgrn, 0), wts['hgrn_lb'][0], wts['hgrn_norm_g'], rev=False, **kw)
    mix_c, sc_b = _hgrn_dir(pc, c_f, sel(s_hgrn, 1), wts['hgrn_lb'][1], wts['hgrn_norm_g'], rev=True, **kw)
    mix_b, sb = _lru(pb, s_lru, wts['lru'], **kw)
    x, y = _out_ffn(x, mix_a, mix_b, mix_c, mods, wts['ffn'], seq_len=seq_len, conv_period=conv_period,
                    final_g=final_g)
    states = None
    if want_final:
        states = (jnp.stack([sa_f, sa_b], axis=1), sb, jnp.stack([sc_f, sc_b], axis=1))
    return x, y, states


def _block_diag(w):
    nd, nb, bi, bj = w.shape
    eye = jnp.eye(nb, dtype=w.dtype)
    return jnp.einsum('dhij,hg->dhigj', w, eye).reshape(nd, nb * bi, nb * bj)


def _pad_dir_rows(w):
    z = jnp.zeros_like(w[0])
    return jnp.stack([jnp.concatenate([w[0], z], axis=0), jnp.concatenate([z, w[1]], axis=0)], axis=0)


def kernel(x_prompt, x_sample, state_rwkv, state_rglru, state_hgrn, c, c_ctx, ada_w, ada_b, norm_mix_g, norm_ffn_g, w_in, w_out, rwkv_w0, rwkv_w_up, rwkv_a0, rwkv_a_up, rwkv_g_up, rwkv_k_k, rwkv_k_a, rwkv_r_k, rwkv_ln_w, rwkv_ln_b, lru_conv_w, lru_conv_b, lru_wa, lru_ba, lru_wx, lru_bx, lru_lambda, hgrn_lb_logits, hgrn_norm_g, ffn_w_gate, ffn_w_up, ffn_conv_w, ffn_conv_b, ffn_w_down, final_g):
    n_ctx, t_ctx, dm = x_prompt.shape
    n_lat, t_lat, _ = x_sample.shape
    depth = w_in.shape[0]
    grid_w = 64

    cond = jnp.concatenate([c_ctx[None, :], c, jnp.zeros((8 - 1 - n_lat, dm), F32)], axis=0)
    mods = _modulation(cond, ada_w, ada_b.reshape(depth, 1, -1))
    lb_all = jnp.cumsum(jax.nn.softmax(hgrn_lb_logits.astype(F32), axis=1), axis=1)
    lb_all = lb_all - lb_all[:, :1]

    xp = x_prompt.reshape(n_ctx * t_ctx, dm)
    xs = x_sample.reshape(n_lat * t_lat, dm)
    new_rwkv, new_lru, new_hgrn = [], [], []
    yp = ys = None
    for l in range(depth):
        wts = {
            'norm_mix_g': norm_mix_g[l],
            'w_in': w_in[l].astype(BF16),
            'rwkv': {'w0': rwkv_w0[l], 'w_up_pad': _pad_dir_rows(rwkv_w_up[l]).astype(BF16),
                     'a0': rwkv_a0[l], 'a_up_pad': _pad_dir_rows(rwkv_a_up[l]).astype(BF16),
                     'g_up': rwkv_g_up[l].astype(BF16), 'k_k': rwkv_k_k[l], 'k_a': rwkv_k_a[l],
                     'r_k': rwkv_r_k[l], 'ln_w': rwkv_ln_w[l], 'ln_b': rwkv_ln_b[l]},
            'lru': {'conv_w': lru_conv_w[l], 'conv_b': lru_conv_b[l],
                    'wa_bd': _block_diag(lru_wa[l]).astype(BF16), 'ba': lru_ba[l],
                    'wx_bd': _block_diag(lru_wx[l]).astype(BF16), 'bx': lru_bx[l], 'lam': lru_lambda[l]},
            'hgrn_lb': lb_all[:, l], 'hgrn_norm_g': hgrn_norm_g[l],
            'ffn': {'norm_g': norm_ffn_g[l], 'w_out': w_out[l].astype(BF16),
                    'w_gate': ffn_w_gate[l].astype(BF16), 'w_up': ffn_w_up[l].astype(BF16),
                    'w_down': ffn_w_down[l].astype(BF16), 'conv_w': ffn_conv_w[l], 'conv_b': ffn_conv_b[l]},
        }
        last = l == depth - 1
        fg = final_g if last else None
        xp, yp, st = _trunk_layer(xp, mods[l, 0:1].reshape(1, 1, -1), None, None, None, wts,
                                  batch=n_ctx, seq_len=t_ctx, conv_period=t_ctx, want_final=True, final_g=fg)
        new_rwkv.append(st[0])
        new_lru.append(st[1])
        new_hgrn.append(st[2])
        xs, ys, _ = _trunk_layer(xs, mods[l, 1:1 + n_lat].reshape(n_lat, 1, -1),
                                 state_rwkv[:, l].astype(F32), state_rglru[:, l].astype(F32),
                                 state_hgrn[:, l].astype(F32), wts,
                                 batch=n_lat, seq_len=t_lat, conv_period=grid_w, want_final=False, final_g=fg)
    return (yp.reshape(n_ctx, t_ctx, dm), ys.reshape(n_lat, t_lat, dm),
            jnp.stack(new_rwkv, axis=1), jnp.stack(new_lru, axis=1), jnp.stack(new_hgrn, axis=1))
```

```python
import functools
import math

import jax
import jax.numpy as jnp
from jax import lax
from jax.experimental import pallas as pl
from jax.experimental.pallas import tpu as pltpu

F32 = jnp.float32
BF16 = jnp.bfloat16

D_MODEL = 1024
HEAD = 64
PAIR = 2 * HEAD
D_A = 512
D_B = 256
D_C = 256
LORA_G = 128
N_MOD = 6
D_FF = 2816
FF_TILE = 256
PA_COLS = 3 * D_A + 4 * 64 + LORA_G
PB_COLS = 2 * D_B
PC_COLS = 5 * D_C
CHUNK = 64
BLOCK_T = 256
PHASE_A_CHUNKS = 2
TOKEN_TILE = 512
RMS_EPS = 1e-6
GN_EPS = 64e-5
RG_C = 8.0
VMEM_LIMIT = 48 * 1024 * 1024


def _dot(a, b):
    return jnp.dot(a.astype(BF16), b.astype(BF16), preferred_element_type=F32)


def _dot_nt(a, b):
    return lax.dot_general(a.astype(BF16), b.astype(BF16), (((1,), (1,)), ((), ())),
                           preferred_element_type=F32)


def _dot_tn(a, b):
    return lax.dot_general(a.astype(BF16), b.astype(BF16), (((0,), (0,)), ((), ())),
                           preferred_element_type=F32)


def _dot_f32(a, b):
    return jnp.dot(a, b, precision=lax.Precision.HIGHEST, preferred_element_type=F32)


def _split3(x):
    hi = x.astype(BF16)
    r1 = x - hi.astype(F32)
    mid = r1.astype(BF16)
    return hi, mid, (r1 - mid.astype(F32)).astype(BF16)


def _dot_left01(m01, x):
    return sum(jnp.dot(m01, t, preferred_element_type=F32) for t in _split3(x))


def _dot_right01(x, m01):
    return sum(jnp.dot(t, m01, preferred_element_type=F32) for t in _split3(x))


def _sigmoid(x):
    return 1.0 / (1.0 + jnp.exp(-x))


def _silu(x):
    return x * _sigmoid(x)


def _softplus(x):
    return jnp.maximum(x, 0.0) + jnp.log1p(jnp.exp(-jnp.abs(x)))


def _iota2(shape, axis):
    return lax.broadcasted_iota(jnp.int32, shape, axis)


def _head_sum_matrix():
    return ((_iota2((PAIR, PAIR), 0) >> 6) == (_iota2((PAIR, PAIR), 1) >> 6)).astype(BF16)


def _head_sums(x, ind):
    return jnp.concatenate([_dot_right01(x[:, i:i + PAIR], ind) for i in range(0, x.shape[1], PAIR)], axis=1)


def _cumsum_matrix(rev):
    t, s = _iota2((CHUNK, CHUNK), 0), _iota2((CHUNK, CHUNK), 1)
    return ((s >= t) if rev else (s <= t)).astype(BF16)


def _stack_heads(x):
    first = _iota2(x.shape, 1) < HEAD
    return jnp.concatenate([jnp.where(first, x, 0.0), jnp.where(first, 0.0, x)], axis=0)


def _fold_heads(x):
    return x[:CHUNK] + x[CHUNK:]


def _pair_masks(rev):
    i, j = _iota2((PAIR, PAIR), 0), _iota2((PAIR, PAIR), 1)
    same = (i >> 6) == (j >> 6)
    t, s = i & (CHUNK - 1), j & (CHUNK - 1)
    strict = same & ((s > t) if rev else (s < t))
    incl = same & ((s >= t) if rev else (s <= t))
    return same, strict, incl, t, s


def _load_state_pairs(s0_ref, s_scr, n_pairs, transpose):
    z = jnp.zeros((HEAD, HEAD), F32)
    for p in range(n_pairs):
        a, b = s0_ref[2 * p], s0_ref[2 * p + 1]
        if transpose:
            a, b = a.T, b.T
        s_scr[p] = jnp.concatenate([jnp.concatenate([a, z], axis=1), jnp.concatenate([z, b], axis=1)], axis=0)


def _store_state_pairs(sfin_ref, s_scr, n_pairs, transpose):
    for p in range(n_pairs):
        s = s_scr[p]
        a, b = s[:HEAD, :HEAD], s[HEAD:, HEAD:]
        if transpose:
            a, b = a.T, b.T
        sfin_ref[2 * p] = a
        sfin_ref[2 * p + 1] = b


def _mod_kernel(c_ref, w_ref, b_ref, o_ref):
    o_ref[...] = _dot_f32(_silu(c_ref[...]), w_ref[...]) + b_ref[...]


def _modulation(cond, ada_w, ada_b):
    n_layer, _, n_out = ada_w.shape
    tn = 512
    return pl.pallas_call(
        _mod_kernel,
        grid=(n_layer, n_out // tn),
        in_specs=[pl.BlockSpec((8, D_MODEL), lambda l, j: (0, 0)),
                  pl.BlockSpec((None, D_MODEL, tn), lambda l, j: (l, 0, j)),
                  pl.BlockSpec((None, 1, tn), lambda l, j: (l, 0, j))],
        out_specs=pl.BlockSpec((None, 8, tn), lambda l, j: (l, 0, j)),
        out_shape=jax.ShapeDtypeStruct((n_layer, 8, n_out), F32),
        name="modulation",
    )(cond, ada_w, ada_b)


def _rms(x):
    return x * lax.rsqrt(jnp.mean(x * x, axis=-1, keepdims=True) + RMS_EPS)


def _norm_in_kernel(x_ref, mod_ref, g_ref, w_ref, pa_ref, pb_ref, pc_ref):
    h = _rms(x_ref[...]) * g_ref[...]
    h = (h * (1.0 + mod_ref[:, D_MODEL:2 * D_MODEL]) + mod_ref[:, 0:D_MODEL]).astype(BF16)
    pa_ref[...] = jnp.dot(h, w_ref[:, 0:PA_COLS], preferred_element_type=F32)
    pb_ref[...] = jnp.dot(h, w_ref[:, PA_COLS:PA_COLS + PB_COLS], preferred_element_type=F32)
    pc_ref[...] = jnp.dot(h, w_ref[:, PA_COLS + PB_COLS:], preferred_element_type=F32)


def _norm_in(x, mods, g, w_in, seq_len):
    n = x.shape[0]
    tiles_per_seq = seq_len // TOKEN_TILE if mods.shape[0] > 1 else None
    mod_map = (lambda i: (i // tiles_per_seq, 0, 0)) if tiles_per_seq else (lambda i: (0, 0, 0))
    const = lambda i: (0, 0)
    return pl.pallas_call(
        _norm_in_kernel,
        grid=(n // TOKEN_TILE,),
        in_specs=[pl.BlockSpec((TOKEN_TILE, D_MODEL), lambda i: (i, 0)),
                  pl.BlockSpec((None, 1, N_MOD * D_MODEL), mod_map),
                  pl.BlockSpec((1, D_MODEL), const),
                  pl.BlockSpec(w_in.shape, const, pipeline_mode=pl.Buffered(1))],
        out_specs=[pl.BlockSpec((TOKEN_TILE, PA_COLS), lambda i: (i, 0)),
                   pl.BlockSpec((TOKEN_TILE, PB_COLS), lambda i: (i, 0)),
                   pl.BlockSpec((TOKEN_TILE, PC_COLS), lambda i: (i, 0))],
        out_shape=[jax.ShapeDtypeStruct((n, PA_COLS), F32),
                   jax.ShapeDtypeStruct((n, PB_COLS), F32),
                   jax.ShapeDtypeStruct((n, PC_COLS), F32)],
        compiler_params=pltpu.CompilerParams(vmem_limit_bytes=VMEM_LIMIT),
        name="norm_in",
    )(x, mods, g, w_in)


def _rwkv_kernel(*refs, rev, has_init, want_final, post, n_blk):
    refs = list(refs)
    pa_ref = refs.pop(0)
    of_ref = refs.pop(0) if post else None
    s0_ref = refs.pop(0) if has_init else None
    w0_ref, wup_ref, a0_ref, aup_ref, kkw_ref, kaw_ref = (refs.pop(0) for _ in range(6))
    if post:
        gup_ref, rk_ref, lnw_ref, lnb_ref = (refs.pop(0) for _ in range(4))
    o_ref = refs.pop(0)
    sfin_ref = refs.pop(0) if want_final else None
    s_scr, kd_scr, kk_scr, beta_scr, logw_scr, q_scr, g_scr, h_scr, o0_scr, dec_scr = (
        refs.pop(0) for _ in range(10))
    ob_scr = refs.pop(0) if post else None

    n_pairs = D_A // PAIR
    j = pl.program_id(1)
    ind = _head_sum_matrix()

    @pl.when(j == 0)
    def _():
        if has_init:
            _load_state_pairs(s0_ref, s_scr, n_pairs, transpose=False)
        else:
            s_scr[...] = jnp.zeros(s_scr.shape, F32)

    k = pa_ref[:, D_A:2 * D_A]
    wl = w0_ref[...] + _dot(jnp.tanh(pa_ref[:, 3 * D_A:3 * D_A + PAIR]), wup_ref[...])
    logw_scr[...] = -_sigmoid(wl) * math.exp(-0.5)
    a = _sigmoid(a0_ref[...] + _dot(pa_ref[:, 3 * D_A + PAIR:3 * D_A + 2 * PAIR], aup_ref[...]))
    kd_scr[...] = k * (1.0 + (a - 1.0) * kaw_ref[...])
    kk = k * kkw_ref[...]
    kk = kk * lax.rsqrt(_head_sums(kk * kk, ind) + 1e-12)
    kk_scr[...] = kk
    beta_scr[...] = kk * a

    tri = _cumsum_matrix(rev)
    _, strict, incl, _, _ = _pair_masks(rev)
    ident = (_iota2((PAIR, PAIR), 0) == _iota2((PAIR, PAIR), 1)).astype(F32)
    out_buf = ob_scr if post else o_ref
    n_chunks = BLOCK_T // CHUNK

    for c0 in range(0, n_chunks, PHASE_A_CHUNKS):
        items = []
        for c in range(c0, c0 + PHASE_A_CHUNKS):
            rows = slice(c * CHUNK, (c + 1) * CHUNK)
            logw = logw_scr[rows, :]
            b = _dot_left01(tri, logw)
            b_last = b[0:1, :] if rev else b[CHUNK - 1:CHUNK, :]
            p_inv = jnp.exp(-b)
            p_end = jnp.exp(b_last - b)
            r_t = pa_ref[rows, 0:D_A] * jnp.exp(b)
            k_t = kk_scr[rows, :] * jnp.exp(b - logw)
            kd = kd_scr[rows, :]
            beta = beta_scr[rows, :]
            k_h, b_h = kd * p_inv, beta * p_inv
            k_e, b_e = kd * p_end, beta * p_end
            v = pa_ref[rows, 2 * D_A:3 * D_A]
            dec_scr[c * 8:c * 8 + 1, :] = jnp.exp(b_last)
            for p in range(n_pairs):
                ln = slice(p * PAIR, (p + 1) * PAIR)
                items.append(dict(
                    idx=c * n_pairs + p,
                    kt_st=_stack_heads(k_t[:, ln]).astype(BF16), rt_st=_stack_heads(r_t[:, ln]),
                    rhs=jnp.concatenate([k_h[:, ln]] * 2 + [b_h[:, ln]] * 2, axis=0).astype(BF16),
                    v_st=_stack_heads(v[:, ln]).astype(BF16),
                    ke_st=_stack_heads(k_e[:, ln]).astype(BF16), be_st=_stack_heads(b_e[:, ln]).astype(BF16)))
        for it in items:
            lhs = jnp.concatenate([it['kt_st'], it['rt_st'].astype(BF16)], axis=0)
            sc = lax.dot_general(lhs, it.pop('rhs'), (((1,), (1,)), ((), ())), preferred_element_type=F32)
            it['a_kr'] = jnp.concatenate([jnp.where(strict, sc[:PAIR, :PAIR], 0.0),
                                          jnp.where(incl, sc[PAIR:, :PAIR], 0.0)], axis=0).astype(BF16)
            it['a_rb'] = jnp.where(incl, sc[PAIR:, PAIR:], 0.0).astype(BF16)
            it['pw'] = jnp.where(strict, -sc[:PAIR, PAIR:], 0.0)
            it['t'] = ident + it['pw']
        for it in items:
            pw16 = it['pw'].astype(BF16)
            it['pw'] = jnp.dot(pw16, pw16, preferred_element_type=F32).astype(BF16)
        for _ in range(4):
            for it in items:
                m = jnp.dot(it['pw'], jnp.concatenate([it['pw'], it['t'].astype(BF16)], axis=1),
                            preferred_element_type=F32)
                it['pw'] = m[:, :PAIR].astype(BF16)
                it['t'] = it['t'] + m[:, PAIR:]
        for it in items:
            it['t'] = it['t'] + jnp.dot(it['t'].astype(BF16), it.pop('pw'), preferred_element_type=F32)
            it['av'] = jnp.dot(it.pop('a_kr'), it['v_st'], preferred_element_type=F32)
        for it in items:
            y0 = it['av'][:PAIR].astype(BF16)
            it['wu'] = jnp.dot(it.pop('t').astype(BF16), jnp.concatenate([it.pop('kt_st'), y0], axis=1),
                               preferred_element_type=F32).astype(BF16)
        for it in items:
            i = it['idx']
            wu = it.pop('wu')
            aw = jnp.dot(it.pop('a_rb'), wu, preferred_element_type=F32)
            q_scr[i] = (it.pop('rt_st') - aw[:, :PAIR]).astype(BF16)
            o0_scr[i] = _fold_heads(it.pop('av')[PAIR:] - aw[:, PAIR:])
            be_st = it.pop('be_st')
            g_scr[i] = lax.dot_general(wu[:, :PAIR], be_st, (((0,), (0,)), ((), ())),
                                       preferred_element_type=F32).astype(BF16)
            h_scr[i] = lax.dot_general(jnp.concatenate([it.pop('v_st'), wu[:, PAIR:]], axis=0),
                                       jnp.concatenate([it.pop('ke_st'), -be_st], axis=0),
                                       (((0,), (0,)), ((), ())), preferred_element_type=F32)

    for c in range(n_chunks):
        cc = (n_chunks - 1 - c) if rev else c
        rows = slice(cc * CHUNK, (cc + 1) * CHUNK)
        for p in range(n_pairs):
            i = cc * n_pairs + p
            ln = slice(p * PAIR, (p + 1) * PAIR)
            s_bd = s_scr[p]
            s16 = s_bd.astype(BF16)
            o_st = lax.dot_general(q_scr[i], s16, (((1,), (1,)), ((), ())), preferred_element_type=F32)
            out_buf[rows, ln] = _fold_heads(o_st) + o0_scr[i]
            s_scr[p] = (s_bd * dec_scr[cc * 8:cc * 8 + 1, ln]
                        - jnp.dot(s16, g_scr[i], preferred_element_type=F32) + h_scr[i])

    if post:
        r = pa_ref[:, 0:D_A]
        o = of_ref[...] + ob_scr[...]
        mu = _head_sums(o, ind) * (1.0 / HEAD)
        d = o - mu
        var = _head_sums(d * d, ind) * (1.0 / HEAD)
        o = d * lax.rsqrt(var + GN_EPS) * lnw_ref[...] + lnb_ref[...]
        bonus = _head_sums(r * pa_ref[:, D_A:2 * D_A] * rk_ref[...], ind) * pa_ref[:, 2 * D_A:3 * D_A]
        gate = _dot(_sigmoid(pa_ref[:, 3 * D_A + 2 * PAIR:3 * D_A + 2 * PAIR + LORA_G]), gup_ref[...])
        o_ref[...] = ((o + bonus) * gate).astype(o_ref.dtype)

    if want_final:
        @pl.when(j == n_blk - 1)
        def _():
            _store_state_pairs(sfin_ref, s_scr, n_pairs, transpose=False)


def _blk_map(n_blk, rev):
    if rev:
        return lambda b, j: (b * n_blk + n_blk - 1 - j, 0)
    return lambda b, j: (b * n_blk + j, 0)


def _row(v):
    return v.reshape(1, -1)


def _rwkv_dir(pa, o_fwd, s0, wts, *, batch, seq_len, rev, want_final):
    n = pa.shape[0]
    n_blk = seq_len // BLOCK_T
    post = rev
    has_init = s0 is not None
    blk = _blk_map(n_blk, rev)
    const = lambda b, j: (0, 0)
    state_spec = pl.BlockSpec((None, D_A // HEAD, HEAD, HEAD), lambda b, j: (b, 0, 0, 0))
    d = 1 if rev else 0
    ins, specs = [pa], [pl.BlockSpec((BLOCK_T, PA_COLS), blk)]
    if post:
        ins.append(o_fwd)
        specs.append(pl.BlockSpec((BLOCK_T, D_A), blk))
    if has_init:
        ins.append(s0)
        specs.append(state_spec)
    small = [_row(wts['w0'][d]), wts['w_up_pad'][d], _row(wts['a0'][d]), wts['a_up_pad'][d],
             _row(wts['k_k']), _row(wts['k_a'])]
    if post:
        small += [wts['g_up'], _row(wts['r_k']), _row(wts['ln_w']), _row(wts['ln_b'])]
    for wgt in small:
        ins.append(wgt)
        specs.append(pl.BlockSpec(wgt.shape, const))
    out_shape = [jax.ShapeDtypeStruct((n, D_A), BF16 if post else F32)]
    out_specs = [pl.BlockSpec((BLOCK_T, D_A), blk)]
    if want_final:
        out_shape.append(jax.ShapeDtypeStruct((batch, D_A // HEAD, HEAD, HEAD), F32))
        out_specs.append(state_spec)
    n_items = (BLOCK_T // CHUNK) * (D_A // PAIR)
    scratch = [pltpu.VMEM((D_A // PAIR, PAIR, PAIR), F32)] + [pltpu.VMEM((BLOCK_T, D_A), F32)] * 4
    scratch += [pltpu.VMEM((n_items, PAIR, PAIR), BF16), pltpu.VMEM((n_items, PAIR, PAIR), BF16),
                pltpu.VMEM((n_items, PAIR, PAIR), F32), pltpu.VMEM((n_items, CHUNK, PAIR), F32),
                pltpu.VMEM((8 * BLOCK_T // CHUNK, D_A), F32)]
    if post:
        scratch.append(pltpu.VMEM((BLOCK_T, D_A), F32))
    res = pl.pallas_call(
        functools.partial(_rwkv_kernel, rev=rev, has_init=has_init, want_final=want_final, post=post,
                          n_blk=n_blk),
        grid=(batch, n_blk),
        in_specs=specs, out_specs=out_specs, out_shape=out_shape, scratch_shapes=scratch,
        compiler_params=pltpu.CompilerParams(vmem_limit_bytes=VMEM_LIMIT),
        name="rwkv_bwd" if rev else "rwkv_fwd",
    )(*ins)
    return res[0], (res[1] if want_final else None)


def _block_edges(b, rev):
    t = _iota2(b.shape, 0)
    e = b
    out = {}
    for kbit in (1, 2, 4, 8, 16):
        if rev:
            e = jnp.where((t & kbit) != 0, pltpu.roll(e, kbit, 0), e)
        else:
            e = jnp.where((t & kbit) == 0, pltpu.roll(e, CHUNK - kbit, 0), e)
        if kbit >= 4:
            out[2 * kbit] = e
    return out


def _hgrn_kernel(*refs, rev, has_init, want_final, post, n_blk):
    refs = list(refs)
    pc_ref = refs.pop(0)
    of_ref = refs.pop(0) if post else None
    s0_ref = refs.pop(0) if has_init else None
    lb_ref = refs.pop(0)
    ng_ref = refs.pop(0) if post else None
    o_ref = refs.pop(0)
    sfin_ref = refs.pop(0) if want_final else None
    s_scr, q_scr, logf_scr, kf_scr = (refs.pop(0) for _ in range(4))
    ob_scr = refs.pop(0) if post else None

    n_pairs = D_C // PAIR
    j = pl.program_id(1)
    ind = _head_sum_matrix()

    @pl.when(j == 0)
    def _():
        if has_init:
            _load_state_pairs(s0_ref, s_scr, n_pairs, transpose=True)
        else:
            s_scr[...] = jnp.zeros(s_scr.shape, F32)

    q_scr[...] = _silu(pc_ref[:, 0:D_C])
    f_col = 2 * D_C if rev else D_C
    lb = lb_ref[...]
    f = lb + (1.0 - lb) * _sigmoid(pc_ref[:, f_col:f_col + D_C])
    logf_scr[...] = jnp.log(f)
    kf_scr[...] = 1.0 - f

    tri = _cumsum_matrix(rev)
    same, _, _, t_idx, s_idx = _pair_masks(rev)
    level_mask = {}
    for h in (8, 16, 32):
        sh = h.bit_length() - 1
        step = -1 if rev else 1
        level_mask[h] = same & ((t_idx >> sh) == (s_idx >> sh) + step) & ((t_idx >> (sh + 1)) == (s_idx >> (sh + 1)))
    t_row = _iota2((CHUNK, PAIR), 0)
    out_buf = ob_scr if post else o_ref

    def chunk_body(c, carry):
        cc = (BLOCK_T // CHUNK - 1 - c) if rev else c
        rows = pl.ds(pl.multiple_of(cc * CHUNK, CHUNK), CHUNK)
        b_all = _dot_left01(tri, logf_scr[rows, :])
        for p in range(n_pairs):
            ln = slice(p * PAIR, (p + 1) * PAIR)
            b = b_all[:, ln]
            b_last = b[0:1, :] if rev else b[CHUNK - 1:CHUNK, :]
            q = q_scr[rows, ln]
            kf = kf_scr[rows, ln]
            v = pc_ref[rows, 3 * D_C + p * PAIR:3 * D_C + (p + 1) * PAIR]
            s_bd = s_scr[p]
            o = _dot_nt(q * jnp.exp(b), s_bd)
            edges = _block_edges(b, rev)
            score = jnp.zeros((PAIR, PAIR), F32)
            for h in (8, 16, 32):
                own = edges[h]
                if rev:
                    partner = jnp.concatenate([own[h:], own[CHUNK - h:]], axis=0)
                else:
                    partner = jnp.concatenate([own[:h], own[:CHUNK - h]], axis=0)
                kp = kf * jnp.exp(jnp.minimum(own - b, 0.0))
                qp = q * jnp.exp(jnp.minimum(b - partner, 0.0))
                sc = _dot_nt(_stack_heads(qp), jnp.concatenate([kp, kp], axis=0))
                score = jnp.where(level_mask[h], sc, score)
            v_st = _stack_heads(v)
            o = o + _fold_heads(_dot(score, v_st))
            for delta in range(8):
                if delta == 0:
                    x, v_s = q * kf, v
                else:
                    sh = (CHUNK - delta) if rev else delta
                    b_s, k_s, v_s = (pltpu.roll(z, sh, 0) for z in (b, kf, v))
                    x = q * jnp.exp(jnp.minimum(b - b_s, 0.0)) * k_s
                    ok = ((t_row & 7) <= 7 - delta) if rev else ((t_row & 7) >= delta)
                    x = jnp.where(ok, x, 0.0)
                o = o + _dot(x, ind) * v_s
            out_buf[rows, ln] = o
            k_e = kf * jnp.exp(b_last - b)
            s_scr[p] = s_bd * jnp.exp(b_last) + _dot_tn(v_st, _stack_heads(k_e))
        return carry

    lax.fori_loop(0, BLOCK_T // CHUNK, chunk_body, 0)

    if post:
        o = of_ref[...] + ob_scr[...]
        ms = _head_sums(o * o, ind) * (1.0 / HEAD)
        o = o * lax.rsqrt(ms + RMS_EPS) * ng_ref[...] * _silu(pc_ref[:, 4 * D_C:5 * D_C])
        o_ref[...] = o.astype(o_ref.dtype)

    if want_final:
        @pl.when(j == n_blk - 1)
        def _():
            _store_state_pairs(sfin_ref, s_scr, n_pairs, transpose=True)


def _hgrn_dir(pc, o_fwd, s0, lb, norm_g, *, batch, seq_len, rev, want_final):
    n = pc.shape[0]
    n_blk = seq_len // BLOCK_T
    post = rev
    has_init = s0 is not None
    blk = _blk_map(n_blk, rev)
    const = lambda b, j: (0, 0)
    state_spec = pl.BlockSpec((None, D_C // HEAD, HEAD, HEAD), lambda b, j: (b, 0, 0, 0))
    ins, specs = [pc], [pl.BlockSpec((BLOCK_T, PC_COLS), blk)]
    if post:
        ins.append(o_fwd)
        specs.append(pl.BlockSpec((BLOCK_T, D_C), blk))
    if has_init:
        ins.append(s0)
        specs.append(state_spec)
    ins.append(_row(lb))
    specs.append(pl.BlockSpec((1, D_C), const))
    if post:
        ins.append(_row(norm_g))
        specs.append(pl.BlockSpec((1, D_C), const))
    out_shape = [jax.ShapeDtypeStruct((n, D_C), BF16 if post else F32)]
    out_specs = [pl.BlockSpec((BLOCK_T, D_C), blk)]
    if want_final:
        out_shape.append(jax.ShapeDtypeStruct((batch, D_C // HEAD, HEAD, HEAD), F32))
        out_specs.append(state_spec)
    scratch = [pltpu.VMEM((D_C // PAIR, PAIR, PAIR), F32)] + [pltpu.VMEM((BLOCK_T, D_C), F32)] * 3
    if post:
        scratch.append(pltpu.VMEM((BLOCK_T, D_C), F32))
    res = pl.pallas_call(
        functools.partial(_hgrn_kernel, rev=rev, has_init=has_init, want_final=want_final, post=post,
                          n_blk=n_blk),
        grid=(batch, n_blk),
        in_specs=specs, out_specs=out_specs, out_shape=out_shape, scratch_shapes=scratch,
        compiler_params=pltpu.CompilerParams(vmem_limit_bytes=VMEM_LIMIT),
        name="hgrn_bwd" if rev else "hgrn_fwd",
    )(*ins)
    return res[0], (res[1] if want_final else None)


def _shift_rows(x, s, t_idx):
    n = x.shape[0]
    y = pltpu.roll(x, s % n, 0)
    return jnp.where((t_idx >= s) if s > 0 else (t_idx < n + s), y, 0.0)


def _linear_scan(a, b, t_idx, rev):
    n = a.shape[0]
    k = 1
    while k < n:
        ok = (t_idx < n - k) if rev else (t_idx >= k)
        sh = (n - k) if rev else k
        b = jnp.where(ok, a * pltpu.roll(b, sh, 0) + b, b)
        a = jnp.where(ok, a * pltpu.roll(a, sh, 0), a)
        k *= 2
    return a, b


def _gelu_tanh(x):
    return 0.5 * x * (1.0 + jnp.tanh(math.sqrt(2.0 / math.pi) * (x + 0.044715 * x * x * x)))


def _lru_kernel(*refs, has_init, want_final):
    refs = list(refs)
    pb_ref = refs.pop(0)
    h0_ref = refs.pop(0) if has_init else None
    cw_ref, cb_ref, wa_ref, ba_ref, wx_ref, bx_ref, lam_ref = (refs.pop(0) for _ in range(7))
    y_ref = refs.pop(0)
    hfin_ref = refs.pop(0) if want_final else None

    n = pb_ref.shape[0]
    t_idx = _iota2((n, D_B), 0)
    xb = pb_ref[:, 0:D_B]
    u = cb_ref[...] + cw_ref[2:3, :] * xb
    u = u + cw_ref[0:1, :] * _shift_rows(xb, 2, t_idx)
    u = u + cw_ref[1:2, :] * _shift_rows(xb, 1, t_idx)
    u = u + cw_ref[3:4, :] * _shift_rows(xb, -1, t_idx)
    h_sum = jnp.zeros((n, D_B), F32)
    for d in range(2):
        rev = d == 1
        rg = _sigmoid(_dot(u, wa_ref[d]) + ba_ref[d:d + 1, :])
        ig = _sigmoid(_dot(u, wx_ref[d]) + bx_ref[d:d + 1, :])
        log_a = -RG_C * rg * _softplus(-lam_ref[d:d + 1, :])
        a_t = jnp.exp(log_a)
        in_scale = jnp.sqrt((1.0 + a_t * a_t) * jnp.tanh(-log_a))
        a_cum, h = _linear_scan(a_t, in_scale * (ig * u), t_idx, rev)
        if has_init:
            h = a_cum * h0_ref[d:d + 1, :] + h
        if want_final:
            hfin_ref[d:d + 1, :] = h[0:1, :] if rev else h[n - 1:n, :]
        h_sum = h_sum + h
    y_ref[...] = (h_sum * _gelu_tanh(pb_ref[:, D_B:2 * D_B])).astype(y_ref.dtype)


def _lru(pb, h0, wts, *, batch, seq_len, want_final):
    n = pb.shape[0]
    has_init = h0 is not None
    const2 = lambda b: (0, 0)
    const3 = lambda b: (0, 0, 0)
    ins, specs = [pb], [pl.BlockSpec((seq_len, PB_COLS), lambda b: (b, 0))]
    state_spec = pl.BlockSpec((None, 2, D_B), lambda b: (b, 0, 0))
    if has_init:
        ins.append(h0)
        specs.append(state_spec)
    for wgt in (wts['conv_w'], _row(wts['conv_b']), wts['wa_bd'], wts['ba'], wts['wx_bd'], wts['bx'],
                wts['lam']):
        ins.append(wgt)
        specs.append(pl.BlockSpec(wgt.shape, const3 if wgt.ndim == 3 else const2))
    out_shape = [jax.ShapeDtypeStruct((n, D_B), BF16)]
    out_specs = [pl.BlockSpec((seq_len, D_B), lambda b: (b, 0))]
    if want_final:
        out_shape.append(jax.ShapeDtypeStruct((batch, 2, D_B), F32))
        out_specs.append(state_spec)
    res = pl.pallas_call(
        functools.partial(_lru_kernel, has_init=has_init, want_final=want_final),
        grid=(batch,),
        in_specs=specs, out_specs=out_specs, out_shape=out_shape,
        compiler_params=pltpu.CompilerParams(vmem_limit_bytes=VMEM_LIMIT),
        name="lru",
    )(*ins)
    return res[0], (res[1] if want_final else None)


def _out_ffn_kernel(*refs, conv_period, final):
    refs = list(refs)
    x_ref, ma_ref, mb_ref, mc_ref, mod_ref, g_ref, wo_ref, wg_ref, wu_ref, wd_ref, cw_ref, cb_ref = (
        refs.pop(0) for _ in range(12))
    fg_ref = refs.pop(0) if final else None
    xo_ref = refs.pop(0)
    yo_ref = refs.pop(0) if final else None

    dm = D_MODEL
    mix = jnp.dot(ma_ref[...], wo_ref[0:D_A, :], preferred_element_type=F32)
    mix = mix + jnp.dot(mb_ref[...], wo_ref[D_A:D_A + D_B, :], preferred_element_type=F32)
    mix = mix + jnp.dot(mc_ref[...], wo_ref[D_A + D_B:, :], preferred_element_type=F32)
    x = x_ref[...] + mod_ref[:, 2 * dm:3 * dm] * mix
    h = _rms(x) * g_ref[...]
    h = (h * (1.0 + mod_ref[:, 4 * dm:5 * dm]) + mod_ref[:, 3 * dm:4 * dm]).astype(BF16)
    tm = x.shape[0]
    t_idx = _iota2((tm, FF_TILE), 0) & (conv_period - 1)
    acc = jnp.zeros((tm, dm), F32)
    for f0 in range(0, D_FF, FF_TILE):
        cols = slice(f0, f0 + FF_TILE)
        g = jnp.dot(h, wg_ref[:, cols], preferred_element_type=F32)
        g_prev = jnp.where(t_idx == 0, 0.0, pltpu.roll(g, 1, 0))
        g_next = jnp.where(t_idx == conv_period - 1, 0.0, pltpu.roll(g, tm - 1, 0))
        g = cw_ref[0:1, cols] * g_prev + cw_ref[1:2, cols] * g + cw_ref[2:3, cols] * g_next + cb_ref[:, cols]
        up = jnp.dot(h, wu_ref[:, cols], preferred_element_type=F32)
        acc = acc + jnp.dot((_silu(g) * up).astype(BF16), wd_ref[cols, :], preferred_element_type=F32)
    x = x + mod_ref[:, 5 * dm:6 * dm] * acc
    xo_ref[...] = x
    if final:
        yo_ref[...] = _rms(x) * fg_ref[...]


def _out_ffn(x, mix_a, mix_b, mix_c, mods, wts, *, seq_len, conv_period, final_g):
    n = x.shape[0]
    final = final_g is not None
    tiles_per_seq = seq_len // TOKEN_TILE if mods.shape[0] > 1 else None
    mod_map = (lambda i: (i // tiles_per_seq, 0, 0)) if tiles_per_seq else (lambda i: (0, 0, 0))
    const = lambda i: (0, 0)
    tok = lambda width: pl.BlockSpec((TOKEN_TILE, width), lambda i: (i, 0))
    resident = lambda w: pl.BlockSpec(w.shape, const, pipeline_mode=pl.Buffered(1))
    ins = [x, mix_a, mix_b, mix_c, mods, _row(wts['norm_g']), wts['w_out'], wts['w_gate'], wts['w_up'],
           wts['w_down'], wts['conv_w'], _row(wts['conv_b'])]
    specs = [tok(D_MODEL), tok(D_A), tok(D_B), tok(D_C), pl.BlockSpec((None, 1, N_MOD * D_MODEL), mod_map),
             pl.BlockSpec((1, D_MODEL), const), resident(wts['w_out']), resident(wts['w_gate']),
             resident(wts['w_up']), resident(wts['w_down']), pl.BlockSpec(wts['conv_w'].shape, const),
             pl.BlockSpec((1, D_FF), const)]
    out_shape = [jax.ShapeDtypeStruct((n, D_MODEL), F32)]
    out_specs = [tok(D_MODEL)]
    if final:
        ins.append(_row(final_g))
        specs.append(pl.BlockSpec((1, D_MODEL), const))
        out_shape.append(jax.ShapeDtypeStruct((n, D_MODEL), F32))
        out_specs.append(tok(D_MODEL))
    res = pl.pallas_call(
        functools.partial(_out_ffn_kernel, conv_period=conv_period, final=final),
        grid=(n // TOKEN_TILE,),
        in_specs=specs, out_specs=out_specs, out_shape=out_shape,
        compiler_params=pltpu.CompilerParams(vmem_limit_bytes=VMEM_LIMIT),
        name="out_ffn",
    )(*ins)
    return res[0], (res[1] if final else None)


def _trunk_layer(x, mods, s_rwkv, s_lru, s_hgrn, wts, *, batch, seq_len, conv_period, want_final, final_g):
    pa, pb, pc = _norm_in(x, mods, _row(wts['norm_mix_g']), wts['w_in'], seq_len)
    kw = dict(batch=batch, seq_len=seq_len, want_final=want_final)
    sel = lambda s, d: None if s is None else s[:, d]
    o_f, sa_f = _rwkv_dir(pa, None, sel(s_rwkv, 0), wts['rwkv'], rev=False, **kw)
    mix_a, sa_b = _rwkv_dir(pa, o_f, sel(s_rwkv, 1), wts['rwkv'], rev=True, **kw)
    c_f, sc_f = _hgrn_dir(pc, None, sel(s_hgrn, 0), wts['hgrn_lb'][0], wts['hgrn_norm_g'], rev=False, **kw)
    mix_c, sc_b = _hgrn_dir(pc, c_f, sel(s_hgrn, 1), wts['hgrn_lb'][1], wts['hgrn_norm_g'], rev=True, **kw)
    mix_b, sb = _lru(pb, s_lru, wts['lru'], **kw)
    x, y = _out_ffn(x, mix_a, mix_b, mix_c, mods, wts['ffn'], seq_len=seq_len, conv_period=conv_period,
                    final_g=final_g)
    states = None
    if want_final:
        states = (jnp.stack([sa_f, sa_b], axis=1), sb, jnp.stack([sc_f, sc_b], axis=1))
    return x, y, states


def _block_diag(w):
    nd, nb, bi, bj = w.shape
    eye = jnp.eye(nb, dtype=w.dtype)
    return jnp.einsum('dhij,hg->dhigj', w, eye).reshape(nd, nb * bi, nb * bj)


def _pad_dir_rows(w):
    z = jnp.zeros_like(w[0])
    return jnp.stack([jnp.concatenate([w[0], z], axis=0), jnp.concatenate([z, w[1]], axis=0)], axis=0)


def kernel(x_prompt, x_sample, state_rwkv, state_rglru, state_hgrn, c, c_ctx, ada_w, ada_b, norm_mix_g, norm_ffn_g, w_in, w_out, rwkv_w0, rwkv_w_up, rwkv_a0, rwkv_a_up, rwkv_g_up, rwkv_k_k, rwkv_k_a, rwkv_r_k, rwkv_ln_w, rwkv_ln_b, lru_conv_w, lru_conv_b, lru_wa, lru_ba, lru_wx, lru_bx, lru_lambda, hgrn_lb_logits, hgrn_norm_g, ffn_w_gate, ffn_w_up, ffn_conv_w, ffn_conv_b, ffn_w_down, final_g):
    n_ctx, t_ctx, dm = x_prompt.shape
    n_lat, t_lat, _ = x_sample.shape
    depth = w_in.shape[0]
    grid_w = 64

    cond = jnp.concatenate([c_ctx[None, :], c, jnp.zeros((8 - 1 - n_lat, dm), F32)], axis=0)
    mods = _modulation(cond, ada_w, ada_b.reshape(depth, 1, -1))
    lb_all = jnp.cumsum(jax.nn.softmax(hgrn_lb_logits.astype(F32), axis=1), axis=1)
    lb_all = lb_all - lb_all[:, :1]

    xp = x_prompt.reshape(n_ctx * t_ctx, dm)
    xs = x_sample.reshape(n_lat * t_lat, dm)
    new_rwkv, new_lru, new_hgrn = [], [], []
    yp = ys = None
    for l in range(depth):
        wts = {
            'norm_mix_g': norm_mix_g[l],
            'w_in': w_in[l].astype(BF16),
            'rwkv': {'w0': rwkv_w0[l], 'w_up_pad': _pad_dir_rows(rwkv_w_up[l]).astype(BF16),
                     'a0': rwkv_a0[l], 'a_up_pad': _pad_dir_rows(rwkv_a_up[l]).astype(BF16),
                     'g_up': rwkv_g_up[l].astype(BF16), 'k_k': rwkv_k_k[l], 'k_a': rwkv_k_a[l],
                     'r_k': rwkv_r_k[l], 'ln_w': rwkv_ln_w[l], 'ln_b': rwkv_ln_b[l]},
            'lru': {'conv_w': lru_conv_w[l], 'conv_b': lru_conv_b[l],
                    'wa_bd': _block_diag(lru_wa[l]).astype(BF16), 'ba': lru_ba[l],
                    'wx_bd': _block_diag(lru_wx[l]).astype(BF16), 'bx': lru_bx[l], 'lam': lru_lambda[l]},
            'hgrn_lb': lb_all[:, l], 'hgrn_norm_g': hgrn_norm_g[l],
            'ffn': {'norm_g': norm_ffn_g[l], 'w_out': w_out[l].astype(BF16),
                    'w_gate': ffn_w_gate[l].astype(BF16), 'w_up': ffn_w_up[l].astype(BF16),
                    'w_down': ffn_w_down[l].astype(BF16), 'conv_w': ffn_conv_w[l], 'conv_b': ffn_conv_b[l]},
        }
        last = l == depth - 1
        fg = final_g if last else None
        xp, yp, st = _trunk_layer(xp, mods[l, 0:1].reshape(1, 1, -1), None, None, None, wts,
                                  batch=n_ctx, seq_len=t_ctx, conv_period=t_ctx, want_final=True, final_g=fg)
        new_rwkv.append(st[0])
        new_lru.append(st[1])
        new_hgrn.append(st[2])
        xs, ys, _ = _trunk_layer(xs, mods[l, 1:1 + n_lat].reshape(n_lat, 1, -1),
                                 state_rwkv[:, l].astype(F32), state_rglru[:, l].astype(F32),
                                 state_hgrn[:, l].astype(F32), wts,
                                 batch=n_lat, seq_len=t_lat, conv_period=grid_w, want_final=False, final_g=fg)
    return (yp.reshape(n_ctx, t_ctx, dm), ys.reshape(n_lat, t_lat, dm),
            jnp.stack(new_rwkv, axis=1), jnp.stack(new_lru, axis=1), jnp.stack(new_hgrn, axis=1))
```

```python
import functools
import math

import jax
import jax.numpy as jnp
from jax import lax
from jax.experimental import pallas as pl
from jax.experimental.pallas import tpu as pltpu

F32 = jnp.float32
BF16 = jnp.bfloat16

D_MODEL = 1024
HEAD = 64
PAIR = 2 * HEAD
D_A = 512
D_B = 256
D_C = 256
LORA_G = 128
N_MOD = 6
D_FF = 2816
FF_TILE = 256
PA_COLS = 3 * D_A + 4 * 64 + LORA_G
PB_COLS = 2 * D_B
PC_COLS = 5 * D_C
CHUNK = 64
BLOCK_T = 256
PHASE_A_CHUNKS = 2
TOKEN_TILE = 512
RMS_EPS = 1e-6
GN_EPS = 64e-5
RG_C = 8.0
VMEM_LIMIT = 48 * 1024 * 1024


def _dot(a, b):
    return jnp.dot(a.astype(BF16), b.astype(BF16), preferred_element_type=F32)


def _dot_nt(a, b):
    return lax.dot_general(a.astype(BF16), b.astype(BF16), (((1,), (1,)), ((), ())),
                           preferred_element_type=F32)


def _dot_tn(a, b):
    return lax.dot_general(a.astype(BF16), b.astype(BF16), (((0,), (0,)), ((), ())),
                           preferred_element_type=F32)


def _dot_f32(a, b):
    return jnp.dot(a, b, precision=lax.Precision.HIGHEST, preferred_element_type=F32)


def _split3(x):
    hi = x.astype(BF16)
    r1 = x - hi.astype(F32)
    mid = r1.astype(BF16)
    return hi, mid, (r1 - mid.astype(F32)).astype(BF16)


def _dot_left01(m01, x):
    return sum(jnp.dot(m01, t, preferred_element_type=F32) for t in _split3(x))


def _dot_right01(x, m01):
    return sum(jnp.dot(t, m01, preferred_element_type=F32) for t in _split3(x))


def _sigmoid(x):
    return 1.0 / (1.0 + jnp.exp(-x))


def _silu(x):
    return x * _sigmoid(x)


def _softplus(x):
    return jnp.maximum(x, 0.0) + jnp.log1p(jnp.exp(-jnp.abs(x)))


def _iota2(shape, axis):
    return lax.broadcasted_iota(jnp.int32, shape, axis)


def _head_sum_matrix():
    return ((_iota2((PAIR, PAIR), 0) >> 6) == (_iota2((PAIR, PAIR), 1) >> 6)).astype(BF16)


def _head_sums(x, ind):
    return jnp.concatenate([_dot_right01(x[:, i:i + PAIR], ind) for i in range(0, x.shape[1], PAIR)], axis=1)


def _cumsum_matrix(rev):
    t, s = _iota2((CHUNK, CHUNK), 0), _iota2((CHUNK, CHUNK), 1)
    return ((s >= t) if rev else (s <= t)).astype(BF16)


def _stack_heads(x):
    first = _iota2(x.shape, 1) < HEAD
    return jnp.concatenate([jnp.where(first, x, 0.0), jnp.where(first, 0.0, x)], axis=0)


def _fold_heads(x):
    return x[:CHUNK] + x[CHUNK:]


def _pair_masks(rev):
    i, j = _iota2((PAIR, PAIR), 0), _iota2((PAIR, PAIR), 1)
    same = (i >> 6) == (j >> 6)
    t, s = i & (CHUNK - 1), j & (CHUNK - 1)
    strict = same & ((s > t) if rev else (s < t))
    incl = same & ((s >= t) if rev else (s <= t))
    return same, strict, incl, t, s


def _load_state_pairs(s0_ref, s_scr, n_pairs, transpose):
    z = jnp.zeros((HEAD, HEAD), F32)
    for p in range(n_pairs):
        a, b = s0_ref[2 * p], s0_ref[2 * p + 1]
        if transpose:
            a, b = a.T, b.T
        s_scr[p] = jnp.concatenate([jnp.concatenate([a, z], axis=1), jnp.concatenate([z, b], axis=1)], axis=0)


def _store_state_pairs(sfin_ref, s_scr, n_pairs, transpose):
    for p in range(n_pairs):
        s = s_scr[p]
        a, b = s[:HEAD, :HEAD], s[HEAD:, HEAD:]
        if transpose:
            a, b = a.T, b.T
        sfin_ref[2 * p] = a
        sfin_ref[2 * p + 1] = b


def _mod_kernel(c_ref, w_ref, b_ref, o_ref):
    o_ref[...] = _dot_f32(_silu(c_ref[...]), w_ref[...]) + b_ref[...]


def _modulation(cond, ada_w, ada_b):
    n_layer, _, n_out = ada_w.shape
    tn = 512
    return pl.pallas_call(
        _mod_kernel,
        grid=(n_layer, n_out // tn),
        in_specs=[pl.BlockSpec((8, D_MODEL), lambda l, j: (0, 0)),
                  pl.BlockSpec((None, D_MODEL, tn), lambda l, j: (l, 0, j)),
                  pl.BlockSpec((None, 1, tn), lambda l, j: (l, 0, j))],
        out_specs=pl.BlockSpec((None, 8, tn), lambda l, j: (l, 0, j)),
        out_shape=jax.ShapeDtypeStruct((n_layer, 8, n_out), F32),
        name="modulation",
    )(cond, ada_w, ada_b)


def _rms(x):
    return x * lax.rsqrt(jnp.mean(x * x, axis=-1, keepdims=True) + RMS_EPS)


def _norm_in_kernel(x_ref, mod_ref, g_ref, w_ref, pa_ref, pb_ref, pc_ref):
    h = _rms(x_ref[...]) * g_ref[...]
    h = (h * (1.0 + mod_ref[:, D_MODEL:2 * D_MODEL]) + mod_ref[:, 0:D_MODEL]).astype(BF16)
    pa_ref[...] = jnp.dot(h, w_ref[:, 0:PA_COLS], preferred_element_type=F32)
    pb_ref[...] = jnp.dot(h, w_ref[:, PA_COLS:PA_COLS + PB_COLS], preferred_element_type=F32)
    pc_ref[...] = jnp.dot(h, w_ref[:, PA_COLS + PB_COLS:], preferred_element_type=F32)


def _norm_in(x, mods, g, w_in, seq_len):
    n = x.shape[0]
    tiles_per_seq = seq_len // TOKEN_TILE if mods.shape[0] > 1 else None
    mod_map = (lambda i: (i // tiles_per_seq, 0, 0)) if tiles_per_seq else (lambda i: (0, 0, 0))
    const = lambda i: (0, 0)
    return pl.pallas_call(
        _norm_in_kernel,
        grid=(n // TOKEN_TILE,),
        in_specs=[pl.BlockSpec((TOKEN_TILE, D_MODEL), lambda i: (i, 0)),
                  pl.BlockSpec((None, 1, N_MOD * D_MODEL), mod_map),
                  pl.BlockSpec((1, D_MODEL), const),
                  pl.BlockSpec(w_in.shape, const, pipeline_mode=pl.Buffered(1))],
        out_specs=[pl.BlockSpec((TOKEN_TILE, PA_COLS), lambda i: (i, 0)),
                   pl.BlockSpec((TOKEN_TILE, PB_COLS), lambda i: (i, 0)),
                   pl.BlockSpec((TOKEN_TILE, PC_COLS), lambda i: (i, 0))],
        out_shape=[jax.ShapeDtypeStruct((n, PA_COLS), F32),
                   jax.ShapeDtypeStruct((n, PB_COLS), F32),
                   jax.ShapeDtypeStruct((n, PC_COLS), F32)],
        compiler_params=pltpu.CompilerParams(vmem_limit_bytes=VMEM_LIMIT),
        name="norm_in",
    )(x, mods, g, w_in)


def _rwkv_kernel(*refs, rev, has_init, want_final, post, n_blk):
    refs = list(refs)
    pa_ref = refs.pop(0)
    of_ref = refs.pop(0) if post else None
    s0_ref = refs.pop(0) if has_init else None
    w0_ref, wup_ref, a0_ref, aup_ref, kkw_ref, kaw_ref = (refs.pop(0) for _ in range(6))
    if post:
        gup_ref, rk_ref, lnw_ref, lnb_ref = (refs.pop(0) for _ in range(4))
    o_ref = refs.pop(0)
    sfin_ref = refs.pop(0) if want_final else None
    s_scr, kd_scr, kk_scr, beta_scr, logw_scr, q_scr, g_scr, h_scr, o0_scr, dec_scr = (
        refs.pop(0) for _ in range(10))
    ob_scr = refs.pop(0) if post else None

    n_pairs = D_A // PAIR
    j = pl.program_id(1)
    ind = _head_sum_matrix()

    @pl.when(j == 0)
    def _():
        if has_init:
            _load_state_pairs(s0_ref, s_scr, n_pairs, transpose=False)
        else:
            s_scr[...] = jnp.zeros(s_scr.shape, F32)

    k = pa_ref[:, D_A:2 * D_A]
    wl = w0_ref[...] + _dot(jnp.tanh(pa_ref[:, 3 * D_A:3 * D_A + PAIR]), wup_ref[...])
    logw_scr[...] = -_sigmoid(wl) * math.exp(-0.5)
    a = _sigmoid(a0_ref[...] + _dot(pa_ref[:, 3 * D_A + PAIR:3 * D_A + 2 * PAIR], aup_ref[...]))
    kd_scr[...] = k * (1.0 + (a - 1.0) * kaw_ref[...])
    kk = k * kkw_ref[...]
    kk = kk * lax.rsqrt(_head_sums(kk * kk, ind) + 1e-12)
    kk_scr[...] = kk
    beta_scr[...] = kk * a

    tri = _cumsum_matrix(rev)
    _, strict, incl, _, _ = _pair_masks(rev)
    ident = (_iota2((PAIR, PAIR), 0) == _iota2((PAIR, PAIR), 1)).astype(F32)
    out_buf = ob_scr if post else o_ref
    n_chunks = BLOCK_T // CHUNK

    for c0 in range(0, n_chunks, PHASE_A_CHUNKS):
        items = []
        for c in range(c0, c0 + PHASE_A_CHUNKS):
            rows = slice(c * CHUNK, (c + 1) * CHUNK)
            logw = logw_scr[rows, :]
            b = _dot_left01(tri, logw)
            b_last = b[0:1, :] if rev else b[CHUNK - 1:CHUNK, :]
            p_inv = jnp.exp(-b)
            p_end = jnp.exp(b_last - b)
            r_t = pa_ref[rows, 0:D_A] * jnp.exp(b)
            k_t = kk_scr[rows, :] * jnp.exp(b - logw)
            kd = kd_scr[rows, :]
            beta = beta_scr[rows, :]
            k_h, b_h = kd * p_inv, beta * p_inv
            k_e, b_e = kd * p_end, beta * p_end
            v = pa_ref[rows, 2 * D_A:3 * D_A]
            dec_scr[c * 8:c * 8 + 1, :] = jnp.exp(b_last)
            for p in range(n_pairs):
                ln = slice(p * PAIR, (p + 1) * PAIR)
                items.append(dict(
                    idx=c * n_pairs + p,
                    kt_st=_stack_heads(k_t[:, ln]).astype(BF16), rt_st=_stack_heads(r_t[:, ln]),
                    rhs=jnp.concatenate([k_h[:, ln]] * 2 + [b_h[:, ln]] * 2, axis=0).astype(BF16),
                    v_st=_stack_heads(v[:, ln]).astype(BF16),
                    ke_st=_stack_heads(k_e[:, ln]).astype(BF16), be_st=_stack_heads(b_e[:, ln]).astype(BF16)))
        for it in items:
            lhs = jnp.concatenate([it['kt_st'], it['rt_st'].astype(BF16)], axis=0)
            sc = lax.dot_general(lhs, it.pop('rhs'), (((1,), (1,)), ((), ())), preferred_element_type=F32)
            it['a_kr'] = jnp.concatenate([jnp.where(strict, sc[:PAIR, :PAIR], 0.0),
                                          jnp.where(incl, sc[PAIR:, :PAIR], 0.0)], axis=0).astype(BF16)
            it['a_rb'] = jnp.where(incl, sc[PAIR:, PAIR:], 0.0).astype(BF16)
            it['pw'] = jnp.where(strict, -sc[:PAIR, PAIR:], 0.0)
            it['t'] = ident + it['pw']
        for it in items:
            pw16 = it['pw'].astype(BF16)
            it['pw'] = jnp.dot(pw16, pw16, preferred_element_type=F32).astype(BF16)
        for _ in range(4):
            for it in items:
                m = jnp.dot(it['pw'], jnp.concatenate([it['pw'], it['t'].astype(BF16)], axis=1),
                            preferred_element_type=F32)
                it['pw'] = m[:, :PAIR].astype(BF16)
                it['t'] = it['t'] + m[:, PAIR:]
        for it in items:
            it['t'] = it['t'] + jnp.dot(it['t'].astype(BF16), it.pop('pw'), preferred_element_type=F32)
            it['av'] = jnp.dot(it.pop('a_kr'), it['v_st'], preferred_element_type=F32)
        for it in items:
            y0 = it['av'][:PAIR].astype(BF16)
            it['wu'] = jnp.dot(it.pop('t').astype(BF16), jnp.concatenate([it.pop('kt_st'), y0], axis=1),
                               preferred_element_type=F32).astype(BF16)
        for it in items:
            i = it['idx']
            wu = it.pop('wu')
            aw = jnp.dot(it.pop('a_rb'), wu, preferred_element_type=F32)
            q_scr[i] = (it.pop('rt_st') - aw[:, :PAIR]).astype(BF16)
            o0_scr[i] = _fold_heads(it.pop('av')[PAIR:] - aw[:, PAIR:])
            be_st = it.pop('be_st')
            g_scr[i] = lax.dot_general(wu[:, :PAIR], be_st, (((0,), (0,)), ((), ())),
                                       preferred_element_type=F32).astype(BF16)
            h_scr[i] = lax.dot_general(jnp.concatenate([it.pop('v_st'), wu[:, PAIR:]], axis=0),
                                       jnp.concatenate([it.pop('ke_st'), -be_st], axis=0),
                                       (((0,), (0,)), ((), ())), preferred_element_type=F32)

    for c in range(n_chunks):
        cc = (n_chunks - 1 - c) if rev else c
        rows = slice(cc * CHUNK, (cc + 1) * CHUNK)
        for p in range(n_pairs):
            i = cc * n_pairs + p
            ln = slice(p * PAIR, (p + 1) * PAIR)
            s_bd = s_scr[p]
            s16 = s_bd.astype(BF16)
            o_st = lax.dot_general(q_scr[i], s16, (((1,), (1,)), ((), ())), preferred_element_type=F32)
            out_buf[rows, ln] = _fold_heads(o_st) + o0_scr[i]
            s_scr[p] = (s_bd * dec_scr[cc * 8:cc * 8 + 1, ln]
                        - jnp.dot(s16, g_scr[i], preferred_element_type=F32) + h_scr[i])

    if post:
        r = pa_ref[:, 0:D_A]
        o = of_ref[...] + ob_scr[...]
        mu = _head_sums(o, ind) * (1.0 / HEAD)
        d = o - mu
        var = _head_sums(d * d, ind) * (1.0 / HEAD)
        o = d * lax.rsqrt(var + GN_EPS) * lnw_ref[...] + lnb_ref[...]
        bonus = _head_sums(r * pa_ref[:, D_A:2 * D_A] * rk_ref[...], ind) * pa_ref[:, 2 * D_A:3 * D_A]
        gate = _dot(_sigmoid(pa_ref[:, 3 * D_A + 2 * PAIR:3 * D_A + 2 * PAIR + LORA_G]), gup_ref[...])
        o_ref[...] = ((o + bonus) * gate).astype(o_ref.dtype)

    if want_final:
        @pl.when(j == n_blk - 1)
        def _():
            _store_state_pairs(sfin_ref, s_scr, n_pairs, transpose=False)


def _blk_map(n_blk, rev):
    if rev:
        return lambda b, j: (b * n_blk + n_blk - 1 - j, 0)
    return lambda b, j: (b * n_blk + j, 0)


def _row(v):
    return v.reshape(1, -1)


def _rwkv_dir(pa, o_fwd, s0, wts, *, batch, seq_len, rev, want_final):
    n = pa.shape[0]
    n_blk = seq_len // BLOCK_T
    post = rev
    has_init = s0 is not None
    blk = _blk_map(n_blk, rev)
    const = lambda b, j: (0, 0)
    state_spec = pl.BlockSpec((None, D_A // HEAD, HEAD, HEAD), lambda b, j: (b, 0, 0, 0))
    d = 1 if rev else 0
    ins, specs = [pa], [pl.BlockSpec((BLOCK_T, PA_COLS), blk)]
    if post:
        ins.append(o_fwd)
        specs.append(pl.BlockSpec((BLOCK_T, D_A), blk))
    if has_init:
        ins.append(s0)
        specs.append(state_spec)
    small = [_row(wts['w0'][d]), wts['w_up_pad'][d], _row(wts['a0'][d]), wts['a_up_pad'][d],
             _row(wts['k_k']), _row(wts['k_a'])]
    if post:
        small += [wts['g_up'], _row(wts['r_k']), _row(wts['ln_w']), _row(wts['ln_b'])]
    for wgt in small:
        ins.append(wgt)
        specs.append(pl.BlockSpec(wgt.shape, const))
    out_shape = [jax.ShapeDtypeStruct((n, D_A), BF16 if post else F32)]
    out_specs = [pl.BlockSpec((BLOCK_T, D_A), blk)]
    if want_final:
        out_shape.append(jax.ShapeDtypeStruct((batch, D_A // HEAD, HEAD, HEAD), F32))
        out_specs.append(state_spec)
    n_items = (BLOCK_T // CHUNK) * (D_A // PAIR)
    scratch = [pltpu.VMEM((D_A // PAIR, PAIR, PAIR), F32)] + [pltpu.VMEM((BLOCK_T, D_A), F32)] * 4
    scratch += [pltpu.VMEM((n_items, PAIR, PAIR), BF16), pltpu.VMEM((n_items, PAIR, PAIR), BF16),
                pltpu.VMEM((n_items, PAIR, PAIR), F32), pltpu.VMEM((n_items, CHUNK, PAIR), F32),
                pltpu.VMEM((8 * BLOCK_T // CHUNK, D_A), F32)]
    if post:
        scratch.append(pltpu.VMEM((BLOCK_T, D_A), F32))
    res = pl.pallas_call(
        functools.partial(_rwkv_kernel, rev=rev, has_init=has_init, want_final=want_final, post=post,
                          n_blk=n_blk),
        grid=(batch, n_blk),
        in_specs=specs, out_specs=out_specs, out_shape=out_shape, scratch_shapes=scratch,
        compiler_params=pltpu.CompilerParams(vmem_limit_bytes=VMEM_LIMIT),
        name="rwkv_bwd" if rev else "rwkv_fwd",
    )(*ins)
    return res[0], (res[1] if want_final else None)


def _block_edges(b, rev):
    t = _iota2(b.shape, 0)
    e = b
    out = {}
    for kbit in (1, 2, 4, 8, 16):
        if rev:
            e = jnp.where((t & kbit) != 0, pltpu.roll(e, kbit, 0), e)
        else:
            e = jnp.where((t & kbit) == 0, pltpu.roll(e, CHUNK - kbit, 0), e)
        if kbit >= 4:
            out[2 * kbit] = e
    return out


def _hgrn_kernel(*refs, rev, has_init, want_final, post, n_blk):
    refs = list(refs)
    pc_ref = refs.pop(0)
    of_ref = refs.pop(0) if post else None
    s0_ref = refs.pop(0) if has_init else None
    lb_ref = refs.pop(0)
    ng_ref = refs.pop(0) if post else None
    o_ref = refs.pop(0)
    sfin_ref = refs.pop(0) if want_final else None
    s_scr, q_scr, logf_scr, kf_scr, f_scr, qe_scr, ds_scr, o0_scr, dec_scr = (refs.pop(0) for _ in range(9))
    ob_scr = refs.pop(0) if post else None

    n_pairs = D_C // PAIR
    j = pl.program_id(1)
    ind = _head_sum_matrix()

    @pl.when(j == 0)
    def _():
        if has_init:
            _load_state_pairs(s0_ref, s_scr, n_pairs, transpose=True)
        else:
            s_scr[...] = jnp.zeros(s_scr.shape, F32)

    q_scr[...] = _silu(pc_ref[:, 0:D_C])
    f_col = 2 * D_C if rev else D_C
    lb = lb_ref[...]
    f = lb + (1.0 - lb) * _sigmoid(pc_ref[:, f_col:f_col + D_C])
    f_scr[...] = f
    logf_scr[...] = jnp.log(f)
    kf_scr[...] = 1.0 - f

    tri = _cumsum_matrix(rev)
    same, _, _, t_idx, s_idx = _pair_masks(rev)
    level_mask = {}
    for h in (8, 16, 32):
        sh = h.bit_length() - 1
        step = -1 if rev else 1
        level_mask[h] = same & ((t_idx >> sh) == (s_idx >> sh) + step) & ((t_idx >> (sh + 1)) == (s_idx >> (sh + 1)))
    t_row = _iota2((CHUNK, PAIR), 0)
    out_buf = ob_scr if post else o_ref

    n_chunks = BLOCK_T // CHUNK

    for c0 in range(0, n_chunks, PHASE_A_CHUNKS):
        items = []
        for c in range(c0, c0 + PHASE_A_CHUNKS):
            rows = slice(c * CHUNK, (c + 1) * CHUNK)
            b_all = _dot_left01(tri, logf_scr[rows, :])
            for p in range(n_pairs):
                ln = slice(p * PAIR, (p + 1) * PAIR)
                b = b_all[:, ln]
                items.append(dict(
                    idx=c * n_pairs + p, b=b, b_last=b[0:1, :] if rev else b[CHUNK - 1:CHUNK, :],
                    q=q_scr[rows, ln], kf=kf_scr[rows, ln], f=f_scr[rows, ln],
                    v=pc_ref[rows, 3 * D_C + p * PAIR:3 * D_C + (p + 1) * PAIR]))
        for it in items:
            b, q, kf = it['b'], it['q'], it['kf']
            edges = _block_edges(b, rev)
            score = jnp.zeros((PAIR, PAIR), F32)
            for h in (8, 16, 32):
                own = edges[h]
                if rev:
                    partner = jnp.concatenate([own[h:], own[CHUNK - h:]], axis=0)
                else:
                    partner = jnp.concatenate([own[:h], own[:CHUNK - h]], axis=0)
                kp = kf * jnp.exp(jnp.minimum(own - b, 0.0))
                qp = q * jnp.exp(jnp.minimum(b - partner, 0.0))
                sc = _dot_nt(_stack_heads(qp), jnp.concatenate([kp, kp], axis=0))
                score = jnp.where(level_mask[h], sc, score)
            it['score'] = score.astype(BF16)
            it['v_st'] = _stack_heads(it['v']).astype(BF16)
        for it in items:
            i = it['idx']
            qe_scr[i] = (it['q'] * jnp.exp(it['b'])).astype(BF16)
            k_e = it['kf'] * jnp.exp(it['b_last'] - it['b'])
            ds_scr[i] = lax.dot_general(it['v_st'], _stack_heads(k_e).astype(BF16), (((0,), (0,)), ((), ())),
                                        preferred_element_type=F32)
            dec_scr[i * 8:i * 8 + 1, :] = jnp.exp(it['b_last'])
            it['o'] = _fold_heads(jnp.dot(it.pop('score'), it.pop('v_st'), preferred_element_type=F32))
        for delta in range(8):
            for it in items:
                q, kf, f, v = it['q'], it['kf'], it['f'], it['v']
                if delta == 0:
                    x, v_s = q * kf, v
                else:
                    roll = lambda z, d: z if d == 0 else pltpu.roll(z, (CHUNK - d) if rev else d, 0)
                    it['e'] = f if delta == 1 else it['e'] * roll(f, delta - 1)
                    x = q * it['e'] * roll(kf, delta)
                    ok = ((t_row & 7) <= 7 - delta) if rev else ((t_row & 7) >= delta)
                    x = jnp.where(ok, x, 0.0)
                    v_s = roll(v, delta)
                it['o'] = it['o'] + _dot(x, ind) * v_s
        for it in items:
            o0_scr[it['idx']] = it['o']

    for c in range(n_chunks):
        cc = (n_chunks - 1 - c) if rev else c
        rows = slice(cc * CHUNK, (cc + 1) * CHUNK)
        for p in range(n_pairs):
            i = cc * n_pairs + p
            s_bd = s_scr[p]
            o = lax.dot_general(qe_scr[i], s_bd.astype(BF16), (((1,), (1,)), ((), ())),
                                preferred_element_type=F32)
            out_buf[rows, p * PAIR:(p + 1) * PAIR] = o + o0_scr[i]
            s_scr[p] = s_bd * dec_scr[i * 8:i * 8 + 1, :] + ds_scr[i]

    if post:
        o = of_ref[...] + ob_scr[...]
        ms = _head_sums(o * o, ind) * (1.0 / HEAD)
        o = o * lax.rsqrt(ms + RMS_EPS) * ng_ref[...] * _silu(pc_ref[:, 4 * D_C:5 * D_C])
        o_ref[...] = o.astype(o_ref.dtype)

    if want_final:
        @pl.when(j == n_blk - 1)
        def _():
            _store_state_pairs(sfin_ref, s_scr, n_pairs, transpose=True)


def _hgrn_dir(pc, o_fwd, s0, lb, norm_g, *, batch, seq_len, rev, want_final):
    n = pc.shape[0]
    n_blk = seq_len // BLOCK_T
    post = rev
    has_init = s0 is not None
    blk = _blk_map(n_blk, rev)
    const = lambda b, j: (0, 0)
    state_spec = pl.BlockSpec((None, D_C // HEAD, HEAD, HEAD), lambda b, j: (b, 0, 0, 0))
    ins, specs = [pc], [pl.BlockSpec((BLOCK_T, PC_COLS), blk)]
    if post:
        ins.append(o_fwd)
        specs.append(pl.BlockSpec((BLOCK_T, D_C), blk))
    if has_init:
        ins.append(s0)
        specs.append(state_spec)
    ins.append(_row(lb))
    specs.append(pl.BlockSpec((1, D_C), const))
    if post:
        ins.append(_row(norm_g))
        specs.append(pl.BlockSpec((1, D_C), const))
    out_shape = [jax.ShapeDtypeStruct((n, D_C), BF16 if post else F32)]
    out_specs = [pl.BlockSpec((BLOCK_T, D_C), blk)]
    if want_final:
        out_shape.append(jax.ShapeDtypeStruct((batch, D_C // HEAD, HEAD, HEAD), F32))
        out_specs.append(state_spec)
    n_items = (BLOCK_T // CHUNK) * (D_C // PAIR)
    scratch = [pltpu.VMEM((D_C // PAIR, PAIR, PAIR), F32)] + [pltpu.VMEM((BLOCK_T, D_C), F32)] * 4
    scratch += [pltpu.VMEM((n_items, CHUNK, PAIR), BF16), pltpu.VMEM((n_items, PAIR, PAIR), F32),
                pltpu.VMEM((n_items, CHUNK, PAIR), F32), pltpu.VMEM((8 * n_items, PAIR), F32)]
    if post:
        scratch.append(pltpu.VMEM((BLOCK_T, D_C), F32))
    res = pl.pallas_call(
        functools.partial(_hgrn_kernel, rev=rev, has_init=has_init, want_final=want_final, post=post,
                          n_blk=n_blk),
        grid=(batch, n_blk),
        in_specs=specs, out_specs=out_specs, out_shape=out_shape, scratch_shapes=scratch,
        compiler_params=pltpu.CompilerParams(vmem_limit_bytes=VMEM_LIMIT),
        name="hgrn_bwd" if rev else "hgrn_fwd",
    )(*ins)
    return res[0], (res[1] if want_final else None)


def _shift_rows(x, s, t_idx):
    n = x.shape[0]
    y = pltpu.roll(x, s % n, 0)
    return jnp.where((t_idx >= s) if s > 0 else (t_idx < n + s), y, 0.0)


def _linear_scan(a, b, t_idx, rev):
    n = a.shape[0]
    k = 1
    while k < n:
        ok = (t_idx < n - k) if rev else (t_idx >= k)
        sh = (n - k) if rev else k
        b = jnp.where(ok, a * pltpu.roll(b, sh, 0) + b, b)
        a = jnp.where(ok, a * pltpu.roll(a, sh, 0), a)
        k *= 2
    return a, b


def _gelu_tanh(x):
    return 0.5 * x * (1.0 + jnp.tanh(math.sqrt(2.0 / math.pi) * (x + 0.044715 * x * x * x)))


def _lru_kernel(*refs, has_init, want_final):
    refs = list(refs)
    pb_ref = refs.pop(0)
    h0_ref = refs.pop(0) if has_init else None
    cw_ref, cb_ref, wa_ref, ba_ref, wx_ref, bx_ref, lam_ref = (refs.pop(0) for _ in range(7))
    y_ref = refs.pop(0)
    hfin_ref = refs.pop(0) if want_final else None

    n = pb_ref.shape[0]
    t_idx = _iota2((n, D_B), 0)
    xb = pb_ref[:, 0:D_B]
    u = cb_ref[...] + cw_ref[2:3, :] * xb
    u = u + cw_ref[0:1, :] * _shift_rows(xb, 2, t_idx)
    u = u + cw_ref[1:2, :] * _shift_rows(xb, 1, t_idx)
    u = u + cw_ref[3:4, :] * _shift_rows(xb, -1, t_idx)
    h_sum = jnp.zeros((n, D_B), F32)
    for d in range(2):
        rev = d == 1
        rg = _sigmoid(_dot(u, wa_ref[d]) + ba_ref[d:d + 1, :])
        ig = _sigmoid(_dot(u, wx_ref[d]) + bx_ref[d:d + 1, :])
        log_a = -RG_C * rg * _softplus(-lam_ref[d:d + 1, :])
        a_t = jnp.exp(log_a)
        in_scale = jnp.sqrt((1.0 + a_t * a_t) * jnp.tanh(-log_a))
        a_cum, h = _linear_scan(a_t, in_scale * (ig * u), t_idx, rev)
        if has_init:
            h = a_cum * h0_ref[d:d + 1, :] + h
        if want_final:
            hfin_ref[d:d + 1, :] = h[0:1, :] if rev else h[n - 1:n, :]
        h_sum = h_sum + h
    y_ref[...] = (h_sum * _gelu_tanh(pb_ref[:, D_B:2 * D_B])).astype(y_ref.dtype)


def _lru(pb, h0, wts, *, batch, seq_len, want_final):
    n = pb.shape[0]
    has_init = h0 is not None
    const2 = lambda b: (0, 0)
    const3 = lambda b: (0, 0, 0)
    ins, specs = [pb], [pl.BlockSpec((seq_len, PB_COLS), lambda b: (b, 0))]
    state_spec = pl.BlockSpec((None, 2, D_B), lambda b: (b, 0, 0))
    if has_init:
        ins.append(h0)
        specs.append(state_spec)
    for wgt in (wts['conv_w'], _row(wts['conv_b']), wts['wa_bd'], wts['ba'], wts['wx_bd'], wts['bx'],
                wts['lam']):
        ins.append(wgt)
        specs.append(pl.BlockSpec(wgt.shape, const3 if wgt.ndim == 3 else const2))
    out_shape = [jax.ShapeDtypeStruct((n, D_B), BF16)]
    out_specs = [pl.BlockSpec((seq_len, D_B), lambda b: (b, 0))]
    if want_final:
        out_shape.append(jax.ShapeDtypeStruct((batch, 2, D_B), F32))
        out_specs.append(state_spec)
    res = pl.pallas_call(
        functools.partial(_lru_kernel, has_init=has_init, want_final=want_final),
        grid=(batch,),
        in_specs=specs, out_specs=out_specs, out_shape=out_shape,
        compiler_params=pltpu.CompilerParams(vmem_limit_bytes=VMEM_LIMIT),
        name="lru",
    )(*ins)
    return res[0], (res[1] if want_final else None)


def _out_ffn_kernel(*refs, conv_period, final):
    refs = list(refs)
    x_ref, ma_ref, mb_ref, mc_ref, mod_ref, g_ref, wo_ref, wg_ref, wu_ref, wd_ref, cw_ref, cb_ref = (
        refs.pop(0) for _ in range(12))
    fg_ref = refs.pop(0) if final else None
    xo_ref = refs.pop(0)
    yo_ref = refs.pop(0) if final else None

    dm = D_MODEL
    mix = jnp.dot(ma_ref[...], wo_ref[0:D_A, :], preferred_element_type=F32)
    mix = mix + jnp.dot(mb_ref[...], wo_ref[D_A:D_A + D_B, :], preferred_element_type=F32)
    mix = mix + jnp.dot(mc_ref[...], wo_ref[D_A + D_B:, :], preferred_element_type=F32)
    x = x_ref[...] + mod_ref[:, 2 * dm:3 * dm] * mix
    h = _rms(x) * g_ref[...]
    h = (h * (1.0 + mod_ref[:, 4 * dm:5 * dm]) + mod_ref[:, 3 * dm:4 * dm]).astype(BF16)
    tm = x.shape[0]
    t_idx = _iota2((tm, FF_TILE), 0) & (conv_period - 1)
    acc = jnp.zeros((tm, dm), F32)
    for f0 in range(0, D_FF, FF_TILE):
        cols = slice(f0, f0 + FF_TILE)
        g = jnp.dot(h, wg_ref[:, cols], preferred_element_type=F32)
        g_prev = jnp.where(t_idx == 0, 0.0, pltpu.roll(g, 1, 0))
        g_next = jnp.where(t_idx == conv_period - 1, 0.0, pltpu.roll(g, tm - 1, 0))
        g = cw_ref[0:1, cols] * g_prev + cw_ref[1:2, cols] * g + cw_ref[2:3, cols] * g_next + cb_ref[:, cols]
        up = jnp.dot(h, wu_ref[:, cols], preferred_element_type=F32)
        acc = acc + jnp.dot((_silu(g) * up).astype(BF16), wd_ref[cols, :], preferred_element_type=F32)
    x = x + mod_ref[:, 5 * dm:6 * dm] * acc
    xo_ref[...] = x
    if final:
        yo_ref[...] = _rms(x) * fg_ref[...]


def _out_ffn(x, mix_a, mix_b, mix_c, mods, wts, *, seq_len, conv_period, final_g):
    n = x.shape[0]
    final = final_g is not None
    tiles_per_seq = seq_len // TOKEN_TILE if mods.shape[0] > 1 else None
    mod_map = (lambda i: (i // tiles_per_seq, 0, 0)) if tiles_per_seq else (lambda i: (0, 0, 0))
    const = lambda i: (0, 0)
    tok = lambda width: pl.BlockSpec((TOKEN_TILE, width), lambda i: (i, 0))
    resident = lambda w: pl.BlockSpec(w.shape, const, pipeline_mode=pl.Buffered(1))
    ins = [x, mix_a, mix_b, mix_c, mods, _row(wts['norm_g']), wts['w_out'], wts['w_gate'], wts['w_up'],
           wts['w_down'], wts['conv_w'], _row(wts['conv_b'])]
    specs = [tok(D_MODEL), tok(D_A), tok(D_B), tok(D_C), pl.BlockSpec((None, 1, N_MOD * D_MODEL), mod_map),
             pl.BlockSpec((1, D_MODEL), const), resident(wts['w_out']), resident(wts['w_gate']),
             resident(wts['w_up']), resident(wts['w_down']), pl.BlockSpec(wts['conv_w'].shape, const),
             pl.BlockSpec((1, D_FF), const)]
    out_shape = [jax.ShapeDtypeStruct((n, D_MODEL), F32)]
    out_specs = [tok(D_MODEL)]
    if final:
        ins.append(_row(final_g))
        specs.append(pl.BlockSpec((1, D_MODEL), const))
        out_shape.append(jax.ShapeDtypeStruct((n, D_MODEL), F32))
        out_specs.append(tok(D_MODEL))
    res = pl.pallas_call(
        functools.partial(_out_ffn_kernel, conv_period=conv_period, final=final),
        grid=(n // TOKEN_TILE,),
        in_specs=specs, out_specs=out_specs, out_shape=out_shape,
        compiler_params=pltpu.CompilerParams(vmem_limit_bytes=VMEM_LIMIT),
        name="out_ffn",
    )(*ins)
    return res[0], (res[1] if final else None)


def _trunk_layer(x, mods, s_rwkv, s_lru, s_hgrn, wts, *, batch, seq_len, conv_period, want_final, final_g):
    pa, pb, pc = _norm_in(x, mods, _row(wts['norm_mix_g']), wts['w_in'], seq_len)
    kw = dict(batch=batch, seq_len=seq_len, want_final=want_final)
    sel = lambda s, d: None if s is None else s[:, d]
    o_f, sa_f = _rwkv_dir(pa, None, sel(s_rwkv, 0), wts['rwkv'], rev=False, **kw)
    mix_a, sa_b = _rwkv_dir(pa, o_f, sel(s_rwkv, 1), wts['rwkv'], rev=True, **kw)
    c_f, sc_f = _hgrn_dir(pc, None, sel(s_hgrn, 0), wts['hgrn_lb'][0], wts['hgrn_norm_g'], rev=False, **kw)
    mix_c, sc_b = _hgrn_dir(pc, c_f, sel(s_hgrn, 1), wts['hgrn_lb'][1], wts['hgrn_norm_g'], rev=True, **kw)
    mix_b, sb = _lru(pb, s_lru, wts['lru'], **kw)
    x, y = _out_ffn(x, mix_a, mix_b, mix_c, mods, wts['ffn'], seq_len=seq_len, conv_period=conv_period,
                    final_g=final_g)
    states = None
    if want_final:
        states = (jnp.stack([sa_f, sa_b], axis=1), sb, jnp.stack([sc_f, sc_b], axis=1))
    return x, y, states


def _block_diag(w):
    nd, nb, bi, bj = w.shape
    eye = jnp.eye(nb, dtype=w.dtype)
    return jnp.einsum('dhij,hg->dhigj', w, eye).reshape(nd, nb * bi, nb * bj)


def _pad_dir_rows(w):
    z = jnp.zeros_like(w[0])
    return jnp.stack([jnp.concatenate([w[0], z], axis=0), jnp.concatenate([z, w[1]], axis=0)], axis=0)


def kernel(x_prompt, x_sample, state_rwkv, state_rglru, state_hgrn, c, c_ctx, ada_w, ada_b, norm_mix_g, norm_ffn_g, w_in, w_out, rwkv_w0, rwkv_w_up, rwkv_a0, rwkv_a_up, rwkv_g_up, rwkv_k_k, rwkv_k_a, rwkv_r_k, rwkv_ln_w, rwkv_ln_b, lru_conv_w, lru_conv_b, lru_wa, lru_ba, lru_wx, lru_bx, lru_lambda, hgrn_lb_logits, hgrn_norm_g, ffn_w_gate, ffn_w_up, ffn_conv_w, ffn_conv_b, ffn_w_down, final_g):
    n_ctx, t_ctx, dm = x_prompt.shape
    n_lat, t_lat, _ = x_sample.shape
    depth = w_in.shape[0]
    grid_w = 64

    cond = jnp.concatenate([c_ctx[None, :], c, jnp.zeros((8 - 1 - n_lat, dm), F32)], axis=0)
    mods = _modulation(cond, ada_w, ada_b.reshape(depth, 1, -1))
    lb_all = jnp.cumsum(jax.nn.softmax(hgrn_lb_logits.astype(F32), axis=1), axis=1)
    lb_all = lb_all - lb_all[:, :1]

    xp = x_prompt.reshape(n_ctx * t_ctx, dm)
    xs = x_sample.reshape(n_lat * t_lat, dm)
    new_rwkv, new_lru, new_hgrn = [], [], []
    yp = ys = None
    for l in range(depth):
        wts = {
            'norm_mix_g': norm_mix_g[l],
            'w_in': w_in[l].astype(BF16),
            'rwkv': {'w0': rwkv_w0[l], 'w_up_pad': _pad_dir_rows(rwkv_w_up[l]).astype(BF16),
                     'a0': rwkv_a0[l], 'a_up_pad': _pad_dir_rows(rwkv_a_up[l]).astype(BF16),
                     'g_up': rwkv_g_up[l].astype(BF16), 'k_k': rwkv_k_k[l], 'k_a': rwkv_k_a[l],
                     'r_k': rwkv_r_k[l], 'ln_w': rwkv_ln_w[l], 'ln_b': rwkv_ln_b[l]},
            'lru': {'conv_w': lru_conv_w[l], 'conv_b': lru_conv_b[l],
                    'wa_bd': _block_diag(lru_wa[l]).astype(BF16), 'ba': lru_ba[l],
                    'wx_bd': _block_diag(lru_wx[l]).astype(BF16), 'bx': lru_bx[l], 'lam': lru_lambda[l]},
            'hgrn_lb': lb_all[:, l], 'hgrn_norm_g': hgrn_norm_g[l],
            'ffn': {'norm_g': norm_ffn_g[l], 'w_out': w_out[l].astype(BF16),
                    'w_gate': ffn_w_gate[l].astype(BF16), 'w_up': ffn_w_up[l].astype(BF16),
                    'w_down': ffn_w_down[l].astype(BF16), 'conv_w': ffn_conv_w[l], 'conv_b': ffn_conv_b[l]},
        }
        last = l == depth - 1
        fg = final_g if last else None
        xp, yp, st = _trunk_layer(xp, mods[l, 0:1].reshape(1, 1, -1), None, None, None, wts,
                                  batch=n_ctx, seq_len=t_ctx, conv_period=t_ctx, want_final=True, final_g=fg)
        new_rwkv.append(st[0])
        new_lru.append(st[1])
        new_hgrn.append(st[2])
        xs, ys, _ = _trunk_layer(xs, mods[l, 1:1 + n_lat].reshape(n_lat, 1, -1),
                                 state_rwkv[:, l].astype(F32), state_rglru[:, l].astype(F32),
                                 state_hgrn[:, l].astype(F32), wts,
                                 batch=n_lat, seq_len=t_lat, conv_period=grid_w, want_final=False, final_g=fg)
    return (yp.reshape(n_ctx, t_ctx, dm), ys.reshape(n_lat, t_lat, dm),
            jnp.stack(new_rwkv, axis=1), jnp.stack(new_lru, axis=1), jnp.stack(new_hgrn, axis=1))
```

```python
import functools
import math

import jax
import jax.numpy as jnp
from jax import lax
from jax.experimental import pallas as pl
from jax.experimental.pallas import tpu as pltpu

F32 = jnp.float32
BF16 = jnp.bfloat16

D_MODEL = 1024
HEAD = 64
PAIR = 2 * HEAD
D_A = 512
D_B = 256
D_C = 256
LORA_G = 128
N_MOD = 6
D_FF = 2816
FF_TILE = 256
PA_COLS = 3 * D_A + 4 * 64 + LORA_G
PB_COLS = 2 * D_B
PC_COLS = 5 * D_C
CHUNK = 64
BLOCK_T = 256
PHASE_A_CHUNKS = 4
TOKEN_TILE = 512
RMS_EPS = 1e-6
GN_EPS = 64e-5
RG_C = 8.0
VMEM_LIMIT = 48 * 1024 * 1024


def _dot(a, b):
    return jnp.dot(a.astype(BF16), b.astype(BF16), preferred_element_type=F32)


def _dot_nt(a, b):
    return lax.dot_general(a.astype(BF16), b.astype(BF16), (((1,), (1,)), ((), ())),
                           preferred_element_type=F32)


def _dot_tn(a, b):
    return lax.dot_general(a.astype(BF16), b.astype(BF16), (((0,), (0,)), ((), ())),
                           preferred_element_type=F32)


def _dot_f32(a, b):
    return jnp.dot(a, b, precision=lax.Precision.HIGHEST, preferred_element_type=F32)


def _split3(x):
    hi = x.astype(BF16)
    r1 = x - hi.astype(F32)
    mid = r1.astype(BF16)
    return hi, mid, (r1 - mid.astype(F32)).astype(BF16)


def _dot_left01(m01, x):
    return sum(jnp.dot(m01, t, preferred_element_type=F32) for t in _split3(x))


def _dot_right01(x, m01):
    hi = x.astype(BF16)
    lo = (x - hi.astype(F32)).astype(BF16)
    return jnp.dot(hi, m01, preferred_element_type=F32) + jnp.dot(lo, m01, preferred_element_type=F32)


def _sigmoid(x):
    return 1.0 / (1.0 + jnp.exp(-x))


def _silu(x):
    return x * _sigmoid(x)


def _softplus(x):
    return jnp.maximum(x, 0.0) + jnp.log1p(jnp.exp(-jnp.abs(x)))


def _iota2(shape, axis):
    return lax.broadcasted_iota(jnp.int32, shape, axis)


def _head_sum_matrix(n=PAIR):
    return ((_iota2((n, n), 0) >> 6) == (_iota2((n, n), 1) >> 6)).astype(BF16)


def _head_sums(x, ind):
    n = ind.shape[0]
    return jnp.concatenate([_dot_right01(x[:, i:i + n], ind) for i in range(0, x.shape[1], n)], axis=1)


def _cumsum_matrix(rev):
    t, s = _iota2((CHUNK, CHUNK), 0), _iota2((CHUNK, CHUNK), 1)
    return ((s >= t) if rev else (s <= t)).astype(BF16)


def _stack_heads(x):
    first = _iota2(x.shape, 1) < HEAD
    return jnp.concatenate([jnp.where(first, x, 0.0), jnp.where(first, 0.0, x)], axis=0)


def _fold_heads(x):
    return x[:CHUNK] + x[CHUNK:]


def _pair_masks(rev):
    i, j = _iota2((PAIR, PAIR), 0), _iota2((PAIR, PAIR), 1)
    same = (i >> 6) == (j >> 6)
    t, s = i & (CHUNK - 1), j & (CHUNK - 1)
    strict = same & ((s > t) if rev else (s < t))
    incl = same & ((s >= t) if rev else (s <= t))
    return same, strict, incl, t, s


def _load_state_pairs(s0_ref, s_scr, n_pairs, transpose):
    z = jnp.zeros((HEAD, HEAD), F32)
    for p in range(n_pairs):
        a, b = s0_ref[2 * p], s0_ref[2 * p + 1]
        if transpose:
            a, b = a.T, b.T
        s_scr[p] = jnp.concatenate([jnp.concatenate([a, z], axis=1), jnp.concatenate([z, b], axis=1)], axis=0)


def _store_state_pairs(sfin_ref, s_scr, n_pairs, transpose):
    for p in range(n_pairs):
        s = s_scr[p]
        a, b = s[:HEAD, :HEAD], s[HEAD:, HEAD:]
        if transpose:
            a, b = a.T, b.T
        sfin_ref[2 * p] = a
        sfin_ref[2 * p + 1] = b


def _mod_kernel(c_ref, w_ref, b_ref, o_ref):
    o_ref[...] = _dot_f32(_silu(c_ref[...]), w_ref[...]) + b_ref[...]


def _modulation(cond, ada_w, ada_b):
    n_layer, _, n_out = ada_w.shape
    tn = 512
    return pl.pallas_call(
        _mod_kernel,
        grid=(n_layer, n_out // tn),
        in_specs=[pl.BlockSpec((8, D_MODEL), lambda l, j: (0, 0)),
                  pl.BlockSpec((None, D_MODEL, tn), lambda l, j: (l, 0, j)),
                  pl.BlockSpec((None, 1, tn), lambda l, j: (l, 0, j))],
        out_specs=pl.BlockSpec((None, 8, tn), lambda l, j: (l, 0, j)),
        out_shape=jax.ShapeDtypeStruct((n_layer, 8, n_out), F32),
        name="modulation",
    )(cond, ada_w, ada_b)


def _rms(x):
    return x * lax.rsqrt(jnp.mean(x * x, axis=-1, keepdims=True) + RMS_EPS)


def _norm_in_kernel(x_ref, mod_ref, g_ref, w_ref, pa_ref, pb_ref, pc_ref):
    h = _rms(x_ref[...]) * g_ref[...]
    h = (h * (1.0 + mod_ref[:, D_MODEL:2 * D_MODEL]) + mod_ref[:, 0:D_MODEL]).astype(BF16)
    pa_ref[...] = jnp.dot(h, w_ref[:, 0:PA_COLS], preferred_element_type=F32)
    pb_ref[...] = jnp.dot(h, w_ref[:, PA_COLS:PA_COLS + PB_COLS], preferred_element_type=F32)
    pc_ref[...] = jnp.dot(h, w_ref[:, PA_COLS + PB_COLS:], preferred_element_type=F32)


def _norm_in(x, mods, g, w_in, seq_len):
    n = x.shape[0]
    tiles_per_seq = seq_len // TOKEN_TILE if mods.shape[0] > 1 else None
    mod_map = (lambda i: (i // tiles_per_seq, 0, 0)) if tiles_per_seq else (lambda i: (0, 0, 0))
    const = lambda i: (0, 0)
    return pl.pallas_call(
        _norm_in_kernel,
        grid=(n // TOKEN_TILE,),
        in_specs=[pl.BlockSpec((TOKEN_TILE, D_MODEL), lambda i: (i, 0)),
                  pl.BlockSpec((None, 1, N_MOD * D_MODEL), mod_map),
                  pl.BlockSpec((1, D_MODEL), const),
                  pl.BlockSpec(w_in.shape, const, pipeline_mode=pl.Buffered(1))],
        out_specs=[pl.BlockSpec((TOKEN_TILE, PA_COLS), lambda i: (i, 0)),
                   pl.BlockSpec((TOKEN_TILE, PB_COLS), lambda i: (i, 0)),
                   pl.BlockSpec((TOKEN_TILE, PC_COLS), lambda i: (i, 0))],
        out_shape=[jax.ShapeDtypeStruct((n, PA_COLS), F32),
                   jax.ShapeDtypeStruct((n, PB_COLS), F32),
                   jax.ShapeDtypeStruct((n, PC_COLS), F32)],
        compiler_params=pltpu.CompilerParams(vmem_limit_bytes=VMEM_LIMIT),
        name="norm_in",
    )(x, mods, g, w_in)


def _rwkv_kernel(*refs, rev, has_init, want_final, post, n_blk):
    refs = list(refs)
    pa_ref = refs.pop(0)
    of_ref = refs.pop(0) if post else None
    s0_ref = refs.pop(0) if has_init else None
    w0_ref, wup_ref, a0_ref, aup_ref, kkw_ref, kaw_ref = (refs.pop(0) for _ in range(6))
    if post:
        gup_ref, rk_ref, lnw_ref, lnb_ref = (refs.pop(0) for _ in range(4))
    o_ref = refs.pop(0)
    sfin_ref = refs.pop(0) if want_final else None
    s_scr, kd_scr, kk_scr, beta_scr, logw_scr, q_scr, g_scr, h_scr, o0_scr, dec_scr = (
        refs.pop(0) for _ in range(10))
    ob_scr = refs.pop(0) if post else None

    n_pairs = D_A // PAIR
    j = pl.program_id(1)
    ind = _head_sum_matrix(2 * PAIR)

    @pl.when(j == 0)
    def _():
        if has_init:
            _load_state_pairs(s0_ref, s_scr, n_pairs, transpose=False)
        else:
            s_scr[...] = jnp.zeros(s_scr.shape, F32)

    k = pa_ref[:, D_A:2 * D_A]
    wl = w0_ref[...] + _dot(jnp.tanh(pa_ref[:, 3 * D_A:3 * D_A + PAIR]), wup_ref[...])
    logw_scr[...] = -_sigmoid(wl) * math.exp(-0.5)
    a = _sigmoid(a0_ref[...] + _dot(pa_ref[:, 3 * D_A + PAIR:3 * D_A + 2 * PAIR], aup_ref[...]))
    kd_scr[...] = k * (1.0 + (a - 1.0) * kaw_ref[...])
    kk = k * kkw_ref[...]
    kk = kk * lax.rsqrt(_head_sums(kk * kk, ind) + 1e-12)
    kk_scr[...] = kk
    beta_scr[...] = kk * a

    tri = _cumsum_matrix(rev)
    same_head = _pair_masks(rev)[0]
    row_t, lane_s = _iota2((CHUNK, PAIR), 0), _iota2((CHUNK, PAIR), 1) & (CHUNK - 1)
    strict_w = (lane_s > row_t) if rev else (lane_s < row_t)
    incl_w = (lane_s >= row_t) if rev else (lane_s <= row_t)
    ident_w = (lane_s == row_t).astype(F32)
    out_buf = ob_scr if post else o_ref
    n_chunks = BLOCK_T // CHUNK

    for c0 in range(0, n_chunks, PHASE_A_CHUNKS):
        items = []
        for c in range(c0, c0 + PHASE_A_CHUNKS):
            rows = slice(c * CHUNK, (c + 1) * CHUNK)
            logw = logw_scr[rows, :]
            b = _dot_left01(tri, logw)
            b_last = b[0:1, :] if rev else b[CHUNK - 1:CHUNK, :]
            p_inv = jnp.exp(-b)
            p_end = jnp.exp(b_last - b)
            r_t = pa_ref[rows, 0:D_A] * jnp.exp(b)
            k_t = kk_scr[rows, :] * jnp.exp(b - logw)
            kd = kd_scr[rows, :]
            beta = beta_scr[rows, :]
            k_h, b_h = kd * p_inv, beta * p_inv
            k_e, b_e = kd * p_end, beta * p_end
            v = pa_ref[rows, 2 * D_A:3 * D_A]
            dec_scr[c * 8:c * 8 + 1, :] = jnp.exp(b_last)
            for p in range(n_pairs):
                ln = slice(p * PAIR, (p + 1) * PAIR)
                items.append(dict(
                    idx=c * n_pairs + p, k_t=k_t[:, ln].astype(BF16), r_t=r_t[:, ln],
                    k_h=k_h[:, ln].astype(BF16), b_h=b_h[:, ln].astype(BF16), v=v[:, ln].astype(BF16),
                    k_e=k_e[:, ln].astype(BF16), b_e=b_e[:, ln].astype(BF16)))
        for it in items:
            lhs = jnp.concatenate([it['k_t'], it['r_t'].astype(BF16)], axis=0)
            rhs = jnp.concatenate([_stack_heads(it.pop('k_h')), _stack_heads(it.pop('b_h'))], axis=0)
            sc = lax.dot_general(lhs, rhs, (((1,), (1,)), ((), ())), preferred_element_type=F32)
            it['a_kr'] = jnp.concatenate([jnp.where(strict_w, sc[:CHUNK, :PAIR], 0.0),
                                          jnp.where(incl_w, sc[CHUNK:, :PAIR], 0.0)], axis=0).astype(BF16)
            it['a_rb'] = jnp.where(incl_w, sc[CHUNK:, PAIR:], 0.0).astype(BF16)
            it['pw'] = jnp.where(strict_w, -sc[:CHUNK, PAIR:], 0.0)
            it['t'] = ident_w + it['pw']
        for it in items:
            pw16 = it['pw'].astype(BF16)
            it['pw'] = jnp.dot(pw16, _stack_heads(pw16), preferred_element_type=F32).astype(BF16)
        for _ in range(4):
            for it in items:
                m = jnp.dot(jnp.concatenate([it['pw'], it['t'].astype(BF16)], axis=0), _stack_heads(it['pw']),
                            preferred_element_type=F32)
                it['pw'] = m[:CHUNK].astype(BF16)
                it['t'] = it['t'] + m[CHUNK:]
        for it in items:
            it['t'] = (it['t'] + jnp.dot(it['t'].astype(BF16), _stack_heads(it.pop('pw')),
                                         preferred_element_type=F32)).astype(BF16)
            it['av'] = jnp.dot(it.pop('a_kr'), _stack_heads(it['v']), preferred_element_type=F32)
        for it in items:
            y0 = it['av'][:CHUNK].astype(BF16)
            rhs = jnp.concatenate([_stack_heads(it.pop('k_t')), _stack_heads(y0)], axis=1)
            it['wu'] = jnp.dot(it.pop('t'), rhs, preferred_element_type=F32).astype(BF16)
        for it in items:
            i = it['idx']
            wu = it.pop('wu')
            w16, u16 = wu[:, :PAIR], wu[:, PAIR:]
            aw = jnp.dot(it.pop('a_rb'), jnp.concatenate([_stack_heads(w16), _stack_heads(u16)], axis=1),
                         preferred_element_type=F32)
            q_scr[i] = (it.pop('r_t') - aw[:, :PAIR]).astype(BF16)
            o0_scr[i] = it.pop('av')[CHUNK:] - aw[:, PAIR:]
            b_e = it.pop('b_e')
            g = lax.dot_general(w16, b_e, (((0,), (0,)), ((), ())), preferred_element_type=F32)
            g_scr[i] = jnp.where(same_head, g, 0.0).astype(BF16)
            h = lax.dot_general(jnp.concatenate([it.pop('v'), u16], axis=0),
                                jnp.concatenate([it.pop('k_e'), -b_e], axis=0),
                                (((0,), (0,)), ((), ())), preferred_element_type=F32)
            h_scr[i] = jnp.where(same_head, h, 0.0)

    for c in range(n_chunks):
        cc = (n_chunks - 1 - c) if rev else c
        rows = slice(cc * CHUNK, (cc + 1) * CHUNK)
        for p in range(n_pairs):
            i = cc * n_pairs + p
            ln = slice(p * PAIR, (p + 1) * PAIR)
            s_bd = s_scr[p]
            s16 = s_bd.astype(BF16)
            o = lax.dot_general(q_scr[i], s16, (((1,), (1,)), ((), ())), preferred_element_type=F32)
            out_buf[rows, ln] = o + o0_scr[i]
            s_scr[p] = (s_bd * dec_scr[cc * 8:cc * 8 + 1, ln]
                        - jnp.dot(s16, g_scr[i], preferred_element_type=F32) + h_scr[i])

    if post:
        r = pa_ref[:, 0:D_A]
        o = of_ref[...] + ob_scr[...]
        mu = _head_sums(o, ind) * (1.0 / HEAD)
        d = o - mu
        var = _head_sums(d * d, ind) * (1.0 / HEAD)
        o = d * lax.rsqrt(var + GN_EPS) * lnw_ref[...] + lnb_ref[...]
        bonus = _head_sums(r * pa_ref[:, D_A:2 * D_A] * rk_ref[...], ind) * pa_ref[:, 2 * D_A:3 * D_A]
        gate = _dot(_sigmoid(pa_ref[:, 3 * D_A + 2 * PAIR:3 * D_A + 2 * PAIR + LORA_G]), gup_ref[...])
        o_ref[...] = ((o + bonus) * gate).astype(o_ref.dtype)

    if want_final:
        @pl.when(j == n_blk - 1)
        def _():
            _store_state_pairs(sfin_ref, s_scr, n_pairs, transpose=False)


def _blk_map(n_blk, rev):
    if rev:
        return lambda b, j: (b * n_blk + n_blk - 1 - j, 0)
    return lambda b, j: (b * n_blk + j, 0)


def _row(v):
    return v.reshape(1, -1)


def _rwkv_dir(pa, o_fwd, s0, wts, *, batch, seq_len, rev, want_final):
    n = pa.shape[0]
    n_blk = seq_len // BLOCK_T
    post = rev
    has_init = s0 is not None
    blk = _blk_map(n_blk, rev)
    const = lambda b, j: (0, 0)
    state_spec = pl.BlockSpec((None, D_A // HEAD, HEAD, HEAD), lambda b, j: (b, 0, 0, 0))
    d = 1 if rev else 0
    ins, specs = [pa], [pl.BlockSpec((BLOCK_T, PA_COLS), blk)]
    if post:
        ins.append(o_fwd)
        specs.append(pl.BlockSpec((BLOCK_T, D_A), blk))
    if has_init:
        ins.append(s0)
        specs.append(state_spec)
    small = [_row(wts['w0'][d]), wts['w_up_pad'][d], _row(wts['a0'][d]), wts['a_up_pad'][d],
             _row(wts['k_k']), _row(wts['k_a'])]
    if post:
        small += [wts['g_up'], _row(wts['r_k']), _row(wts['ln_w']), _row(wts['ln_b'])]
    for wgt in small:
        ins.append(wgt)
        specs.append(pl.BlockSpec(wgt.shape, const))
    out_shape = [jax.ShapeDtypeStruct((n, D_A), BF16 if post else F32)]
    out_specs = [pl.BlockSpec((BLOCK_T, D_A), blk)]
    if want_final:
        out_shape.append(jax.ShapeDtypeStruct((batch, D_A // HEAD, HEAD, HEAD), F32))
        out_specs.append(state_spec)
    n_items = (BLOCK_T // CHUNK) * (D_A // PAIR)
    scratch = [pltpu.VMEM((D_A // PAIR, PAIR, PAIR), F32)] + [pltpu.VMEM((BLOCK_T, D_A), F32)] * 4
    scratch += [pltpu.VMEM((n_items, CHUNK, PAIR), BF16), pltpu.VMEM((n_items, PAIR, PAIR), BF16),
                pltpu.VMEM((n_items, PAIR, PAIR), F32), pltpu.VMEM((n_items, CHUNK, PAIR), F32),
                pltpu.VMEM((8 * BLOCK_T // CHUNK, D_A), F32)]
    if post:
        scratch.append(pltpu.VMEM((BLOCK_T, D_A), F32))
    res = pl.pallas_call(
        functools.partial(_rwkv_kernel, rev=rev, has_init=has_init, want_final=want_final, post=post,
                          n_blk=n_blk),
        grid=(batch, n_blk),
        in_specs=specs, out_specs=out_specs, out_shape=out_shape, scratch_shapes=scratch,
        compiler_params=pltpu.CompilerParams(vmem_limit_bytes=VMEM_LIMIT),
        name="rwkv_bwd" if rev else "rwkv_fwd",
    )(*ins)
    return res[0], (res[1] if want_final else None)


def _block_edges(b, rev):
    t = _iota2(b.shape, 0)
    e = b
    out = {}
    for kbit in (1, 2, 4, 8, 16):
        if rev:
            e = jnp.where((t & kbit) != 0, pltpu.roll(e, kbit, 0), e)
        else:
            e = jnp.where((t & kbit) == 0, pltpu.roll(e, CHUNK - kbit, 0), e)
        if kbit >= 4:
            out[2 * kbit] = e
    return out


def _hgrn_kernel(*refs, rev, has_init, want_final, post, n_blk):
    refs = list(refs)
    pc_ref = refs.pop(0)
    of_ref = refs.pop(0) if post else None
    s0_ref = refs.pop(0) if has_init else None
    lb_ref = refs.pop(0)
    ng_ref = refs.pop(0) if post else None
    o_ref = refs.pop(0)
    sfin_ref = refs.pop(0) if want_final else None
    s_scr, q_scr, logf_scr, kf_scr, f_scr, qe_scr, ds_scr, o0_scr, dec_scr = (refs.pop(0) for _ in range(9))
    ob_scr = refs.pop(0) if post else None

    n_pairs = D_C // PAIR
    j = pl.program_id(1)
    ind = _head_sum_matrix()

    @pl.when(j == 0)
    def _():
        if has_init:
            _load_state_pairs(s0_ref, s_scr, n_pairs, transpose=True)
        else:
            s_scr[...] = jnp.zeros(s_scr.shape, F32)

    q_scr[...] = _silu(pc_ref[:, 0:D_C])
    f_col = 2 * D_C if rev else D_C
    lb = lb_ref[...]
    f = lb + (1.0 - lb) * _sigmoid(pc_ref[:, f_col:f_col + D_C])
    f_scr[...] = f
    logf_scr[...] = jnp.log(f)
    kf_scr[...] = 1.0 - f

    tri = _cumsum_matrix(rev)
    same, _, _, t_idx, s_idx = _pair_masks(rev)
    level_mask = {}
    for h in (8, 16, 32):
        sh = h.bit_length() - 1
        step = -1 if rev else 1
        level_mask[h] = same & ((t_idx >> sh) == (s_idx >> sh) + step) & ((t_idx >> (sh + 1)) == (s_idx >> (sh + 1)))
    t_row = _iota2((CHUNK, PAIR), 0)
    out_buf = ob_scr if post else o_ref

    n_chunks = BLOCK_T // CHUNK

    for c0 in range(0, n_chunks, PHASE_A_CHUNKS):
        items = []
        for c in range(c0, c0 + PHASE_A_CHUNKS):
            rows = slice(c * CHUNK, (c + 1) * CHUNK)
            b_all = _dot_left01(tri, logf_scr[rows, :])
            for p in range(n_pairs):
                ln = slice(p * PAIR, (p + 1) * PAIR)
                b = b_all[:, ln]
                items.append(dict(
                    idx=c * n_pairs + p, b=b, b_last=b[0:1, :] if rev else b[CHUNK - 1:CHUNK, :],
                    q=q_scr[rows, ln], kf=kf_scr[rows, ln], f=f_scr[rows, ln],
                    v=pc_ref[rows, 3 * D_C + p * PAIR:3 * D_C + (p + 1) * PAIR]))
        for it in items:
            b, q, kf = it['b'], it['q'], it['kf']
            edges = _block_edges(b, rev)
            score = jnp.zeros((PAIR, PAIR), F32)
            for h in (8, 16, 32):
                own = edges[h]
                if rev:
                    partner = jnp.concatenate([own[h:], own[CHUNK - h:]], axis=0)
                else:
                    partner = jnp.concatenate([own[:h], own[:CHUNK - h]], axis=0)
                kp = kf * jnp.exp(jnp.minimum(own - b, 0.0))
                qp = q * jnp.exp(jnp.minimum(b - partner, 0.0))
                sc = _dot_nt(_stack_heads(qp), jnp.concatenate([kp, kp], axis=0))
                score = jnp.where(level_mask[h], sc, score)
            it['score'] = score.astype(BF16)
            it['v_st'] = _stack_heads(it['v']).astype(BF16)
        for it in items:
            i = it['idx']
            qe_scr[i] = (it['q'] * jnp.exp(it['b'])).astype(BF16)
            k_e = it['kf'] * jnp.exp(it['b_last'] - it['b'])
            ds_scr[i] = lax.dot_general(it['v_st'], _stack_heads(k_e).astype(BF16), (((0,), (0,)), ((), ())),
                                        preferred_element_type=F32)
            dec_scr[i * 8:i * 8 + 1, :] = jnp.exp(it['b_last'])
            it['o'] = _fold_heads(jnp.dot(it.pop('score'), it.pop('v_st'), preferred_element_type=F32))
        for delta in range(8):
            for it in items:
                q, kf, f, v = it['q'], it['kf'], it['f'], it['v']
                if delta == 0:
                    x, v_s = q * kf, v
                else:
                    roll = lambda z, d: z if d == 0 else pltpu.roll(z, (CHUNK - d) if rev else d, 0)
                    it['e'] = f if delta == 1 else it['e'] * roll(f, delta - 1)
                    x = q * it['e'] * roll(kf, delta)
                    ok = ((t_row & 7) <= 7 - delta) if rev else ((t_row & 7) >= delta)
                    x = jnp.where(ok, x, 0.0)
                    v_s = roll(v, delta)
                it['o'] = it['o'] + _dot(x, ind) * v_s
        for it in items:
            o0_scr[it['idx']] = it['o']

    for c in range(n_chunks):
        cc = (n_chunks - 1 - c) if rev else c
        rows = slice(cc * CHUNK, (cc + 1) * CHUNK)
        for p in range(n_pairs):
            i = cc * n_pairs + p
            s_bd = s_scr[p]
            o = lax.dot_general(qe_scr[i], s_bd.astype(BF16), (((1,), (1,)), ((), ())),
                                preferred_element_type=F32)
            out_buf[rows, p * PAIR:(p + 1) * PAIR] = o + o0_scr[i]
            s_scr[p] = s_bd * dec_scr[i * 8:i * 8 + 1, :] + ds_scr[i]

    if post:
        o = of_ref[...] + ob_scr[...]
        ms = _head_sums(o * o, _head_sum_matrix(D_C)) * (1.0 / HEAD)
        o = o * lax.rsqrt(ms + RMS_EPS) * ng_ref[...] * _silu(pc_ref[:, 4 * D_C:5 * D_C])
        o_ref[...] = o.astype(o_ref.dtype)

    if want_final:
        @pl.when(j == n_blk - 1)
        def _():
            _store_state_pairs(sfin_ref, s_scr, n_pairs, transpose=True)


def _hgrn_dir(pc, o_fwd, s0, lb, norm_g, *, batch, seq_len, rev, want_final):
    n = pc.shape[0]
    n_blk = seq_len // BLOCK_T
    post = rev
    has_init = s0 is not None
    blk = _blk_map(n_blk, rev)
    const = lambda b, j: (0, 0)
    state_spec = pl.BlockSpec((None, D_C // HEAD, HEAD, HEAD), lambda b, j: (b, 0, 0, 0))
    ins, specs = [pc], [pl.BlockSpec((BLOCK_T, PC_COLS), blk)]
    if post:
        ins.append(o_fwd)
        specs.append(pl.BlockSpec((BLOCK_T, D_C), blk))
    if has_init:
        ins.append(s0)
        specs.append(state_spec)
    ins.append(_row(lb))
    specs.append(pl.BlockSpec((1, D_C), const))
    if post:
        ins.append(_row(norm_g))
        specs.append(pl.BlockSpec((1, D_C), const))
    out_shape = [jax.ShapeDtypeStruct((n, D_C), BF16 if post else F32)]
    out_specs = [pl.BlockSpec((BLOCK_T, D_C), blk)]
    if want_final:
        out_shape.append(jax.ShapeDtypeStruct((batch, D_C // HEAD, HEAD, HEAD), F32))
        out_specs.append(state_spec)
    n_items = (BLOCK_T // CHUNK) * (D_C // PAIR)
    scratch = [pltpu.VMEM((D_C // PAIR, PAIR, PAIR), F32)] + [pltpu.VMEM((BLOCK_T, D_C), F32)] * 4
    scratch += [pltpu.VMEM((n_items, CHUNK, PAIR), BF16), pltpu.VMEM((n_items, PAIR, PAIR), F32),
                pltpu.VMEM((n_items, CHUNK, PAIR), F32), pltpu.VMEM((8 * n_items, PAIR), F32)]
    if post:
        scratch.append(pltpu.VMEM((BLOCK_T, D_C), F32))
    res = pl.pallas_call(
        functools.partial(_hgrn_kernel, rev=rev, has_init=has_init, want_final=want_final, post=post,
                          n_blk=n_blk),
        grid=(batch, n_blk),
        in_specs=specs, out_specs=out_specs, out_shape=out_shape, scratch_shapes=scratch,
        compiler_params=pltpu.CompilerParams(vmem_limit_bytes=VMEM_LIMIT),
        name="hgrn_bwd" if rev else "hgrn_fwd",
    )(*ins)
    return res[0], (res[1] if want_final else None)


def _shift_rows(x, s, t_idx):
    n = x.shape[0]
    y = pltpu.roll(x, s % n, 0)
    return jnp.where((t_idx >= s) if s > 0 else (t_idx < n + s), y, 0.0)


def _linear_scan(a, b, t_idx, rev):
    n = a.shape[0]
    k = 1
    while k < n:
        ok = (t_idx < n - k) if rev else (t_idx >= k)
        sh = (n - k) if rev else k
        b = jnp.where(ok, a * pltpu.roll(b, sh, 0) + b, b)
        a = jnp.where(ok, a * pltpu.roll(a, sh, 0), a)
        k *= 2
    return a, b


def _gelu_tanh(x):
    return 0.5 * x * (1.0 + jnp.tanh(math.sqrt(2.0 / math.pi) * (x + 0.044715 * x * x * x)))


def _lru_kernel(*refs, has_init, want_final):
    refs = list(refs)
    pb_ref = refs.pop(0)
    h0_ref = refs.pop(0) if has_init else None
    cw_ref, cb_ref, wa_ref, ba_ref, wx_ref, bx_ref, lam_ref = (refs.pop(0) for _ in range(7))
    y_ref = refs.pop(0)
    hfin_ref = refs.pop(0) if want_final else None

    n = pb_ref.shape[0]
    t_idx = _iota2((n, D_B), 0)
    xb = pb_ref[:, 0:D_B]
    u = cb_ref[...] + cw_ref[2:3, :] * xb
    u = u + cw_ref[0:1, :] * _shift_rows(xb, 2, t_idx)
    u = u + cw_ref[1:2, :] * _shift_rows(xb, 1, t_idx)
    u = u + cw_ref[3:4, :] * _shift_rows(xb, -1, t_idx)
    h_sum = jnp.zeros((n, D_B), F32)
    for d in range(2):
        rev = d == 1
        rg = _sigmoid(_dot(u, wa_ref[d]) + ba_ref[d:d + 1, :])
        ig = _sigmoid(_dot(u, wx_ref[d]) + bx_ref[d:d + 1, :])
        log_a = -RG_C * rg * _softplus(-lam_ref[d:d + 1, :])
        a_t = jnp.exp(log_a)
        in_scale = jnp.sqrt((1.0 + a_t * a_t) * jnp.tanh(-log_a))
        a_cum, h = _linear_scan(a_t, in_scale * (ig * u), t_idx, rev)
        if has_init:
            h = a_cum * h0_ref[d:d + 1, :] + h
        if want_final:
            hfin_ref[d:d + 1, :] = h[0:1, :] if rev else h[n - 1:n, :]
        h_sum = h_sum + h
    y_ref[...] = (h_sum * _gelu_tanh(pb_ref[:, D_B:2 * D_B])).astype(y_ref.dtype)


def _lru(pb, h0, wts, *, batch, seq_len, want_final):
    n = pb.shape[0]
    has_init = h0 is not None
    const2 = lambda b: (0, 0)
    const3 = lambda b: (0, 0, 0)
    ins, specs = [pb], [pl.BlockSpec((seq_len, PB_COLS), lambda b: (b, 0))]
    state_spec = pl.BlockSpec((None, 2, D_B), lambda b: (b, 0, 0))
    if has_init:
        ins.append(h0)
        specs.append(state_spec)
    for wgt in (wts['conv_w'], _row(wts['conv_b']), wts['wa_bd'], wts['ba'], wts['wx_bd'], wts['bx'],
                wts['lam']):
        ins.append(wgt)
        specs.append(pl.BlockSpec(wgt.shape, const3 if wgt.ndim == 3 else const2))
    out_shape = [jax.ShapeDtypeStruct((n, D_B), BF16)]
    out_specs = [pl.BlockSpec((seq_len, D_B), lambda b: (b, 0))]
    if want_final:
        out_shape.append(jax.ShapeDtypeStruct((batch, 2, D_B), F32))
        out_specs.append(state_spec)
    res = pl.pallas_call(
        functools.partial(_lru_kernel, has_init=has_init, want_final=want_final),
        grid=(batch,),
        in_specs=specs, out_specs=out_specs, out_shape=out_shape,
        compiler_params=pltpu.CompilerParams(vmem_limit_bytes=VMEM_LIMIT),
        name="lru",
    )(*ins)
    return res[0], (res[1] if want_final else None)


def _out_ffn_kernel(*refs, conv_period, final):
    refs = list(refs)
    x_ref, ma_ref, mb_ref, mc_ref, mod_ref, g_ref, wo_ref, wg_ref, wu_ref, wd_ref, cw_ref, cb_ref = (
        refs.pop(0) for _ in range(12))
    fg_ref = refs.pop(0) if final else None
    xo_ref = refs.pop(0)
    yo_ref = refs.pop(0) if final else None

    dm = D_MODEL
    mix = jnp.dot(ma_ref[...], wo_ref[0:D_A, :], preferred_element_type=F32)
    mix = mix + jnp.dot(mb_ref[...], wo_ref[D_A:D_A + D_B, :], preferred_element_type=F32)
    mix = mix + jnp.dot(mc_ref[...], wo_ref[D_A + D_B:, :], preferred_element_type=F32)
    x = x_ref[...] + mod_ref[:, 2 * dm:3 * dm] * mix
    h = _rms(x) * g_ref[...]
    h = (h * (1.0 + mod_ref[:, 4 * dm:5 * dm]) + mod_ref[:, 3 * dm:4 * dm]).astype(BF16)
    tm = x.shape[0]
    t_idx = _iota2((tm, FF_TILE), 0) & (conv_period - 1)
    acc = jnp.zeros((tm, dm), F32)
    for f0 in range(0, D_FF, FF_TILE):
        cols = slice(f0, f0 + FF_TILE)
        g = jnp.dot(h, wg_ref[:, cols], preferred_element_type=F32)
        g_prev = jnp.where(t_idx == 0, 0.0, pltpu.roll(g, 1, 0))
        g_next = jnp.where(t_idx == conv_period - 1, 0.0, pltpu.roll(g, tm - 1, 0))
        g = cw_ref[0:1, cols] * g_prev + cw_ref[1:2, cols] * g + cw_ref[2:3, cols] * g_next + cb_ref[:, cols]
        up = jnp.dot(h, wu_ref[:, cols], preferred_element_type=F32)
        acc = acc + jnp.dot((_silu(g) * up).astype(BF16), wd_ref[cols, :], preferred_element_type=F32)
    x = x + mod_ref[:, 5 * dm:6 * dm] * acc
    xo_ref[...] = x
    if final:
        yo_ref[...] = _rms(x) * fg_ref[...]


def _out_ffn(x, mix_a, mix_b, mix_c, mods, wts, *, seq_len, conv_period, final_g):
    n = x.shape[0]
    final = final_g is not None
    tiles_per_seq = seq_len // TOKEN_TILE if mods.shape[0] > 1 else None
    mod_map = (lambda i: (i // tiles_per_seq, 0, 0)) if tiles_per_seq else (lambda i: (0, 0, 0))
    const = lambda i: (0, 0)
    tok = lambda width: pl.BlockSpec((TOKEN_TILE, width), lambda i: (i, 0))
    resident = lambda w: pl.BlockSpec(w.shape, const, pipeline_mode=pl.Buffered(1))
    ins = [x, mix_a, mix_b, mix_c, mods, _row(wts['norm_g']), wts['w_out'], wts['w_gate'], wts['w_up'],
           wts['w_down'], wts['conv_w'], _row(wts['conv_b'])]
    specs = [tok(D_MODEL), tok(D_A), tok(D_B), tok(D_C), pl.BlockSpec((None, 1, N_MOD * D_MODEL), mod_map),
             pl.BlockSpec((1, D_MODEL), const), resident(wts['w_out']), resident(wts['w_gate']),
             resident(wts['w_up']), resident(wts['w_down']), pl.BlockSpec(wts['conv_w'].shape, const),
             pl.BlockSpec((1, D_FF), const)]
    out_shape = [jax.ShapeDtypeStruct((n, D_MODEL), F32)]
    out_specs = [tok(D_MODEL)]
    if final:
        ins.append(_row(final_g))
        specs.append(pl.BlockSpec((1, D_MODEL), const))
        out_shape.append(jax.ShapeDtypeStruct((n, D_MODEL), F32))
        out_specs.append(tok(D_MODEL))
    res = pl.pallas_call(
        functools.partial(_out_ffn_kernel, conv_period=conv_period, final=final),
        grid=(n // TOKEN_TILE,),
        in_specs=specs, out_specs=out_specs, out_shape=out_shape,
        compiler_params=pltpu.CompilerParams(vmem_limit_bytes=VMEM_LIMIT),
        name="out_ffn",
    )(*ins)
    return res[0], (res[1] if final else None)


def _trunk_layer(x, mods, s_rwkv, s_lru, s_hgrn, wts, *, batch, seq_len, conv_period, want_final, final_g):
    pa, pb, pc = _norm_in(x, mods, _row(wts['norm_mix_g']), wts['w_in'], seq_len)
    kw = dict(batch=batch, seq_len=seq_len, want_final=want_final)
    sel = lambda s, d: None if s is None else s[:, d]
    o_f, sa_f = _rwkv_dir(pa, None, sel(s_rwkv, 0), wts['rwkv'], rev=False, **kw)
    mix_a, sa_b = _rwkv_dir(pa, o_f, sel(s_rwkv, 1), wts['rwkv'], rev=True, **kw)
    c_f, sc_f = _hgrn_dir(pc, None, sel(s_hgrn, 0), wts['hgrn_lb'][0], wts['hgrn_norm_g'], rev=False, **kw)
    mix_c, sc_b = _hgrn_dir(pc, c_f, sel(s_hgrn, 1), wts['hgrn_lb'][1], wts['hgrn_norm_g'], rev=True, **kw)
    mix_b, sb = _lru(pb, s_lru, wts['lru'], **kw)
    x, y = _out_ffn(x, mix_a, mix_b, mix_c, mods, wts['ffn'], seq_len=seq_len, conv_period=conv_period,
                    final_g=final_g)
    states = None
    if want_final:
        states = (jnp.stack([sa_f, sa_b], axis=1), sb, jnp.stack([sc_f, sc_b], axis=1))
    return x, y, states


def _block_diag(w):
    nd, nb, bi, bj = w.shape
    eye = jnp.eye(nb, dtype=w.dtype)
    return jnp.einsum('dhij,hg->dhigj', w, eye).reshape(nd, nb * bi, nb * bj)


def _pad_dir_rows(w):
    z = jnp.zeros_like(w[0])
    return jnp.stack([jnp.concatenate([w[0], z], axis=0), jnp.concatenate([z, w[1]], axis=0)], axis=0)


def kernel(x_prompt, x_sample, state_rwkv, state_rglru, state_hgrn, c, c_ctx, ada_w, ada_b, norm_mix_g, norm_ffn_g, w_in, w_out, rwkv_w0, rwkv_w_up, rwkv_a0, rwkv_a_up, rwkv_g_up, rwkv_k_k, rwkv_k_a, rwkv_r_k, rwkv_ln_w, rwkv_ln_b, lru_conv_w, lru_conv_b, lru_wa, lru_ba, lru_wx, lru_bx, lru_lambda, hgrn_lb_logits, hgrn_norm_g, ffn_w_gate, ffn_w_up, ffn_conv_w, ffn_conv_b, ffn_w_down, final_g):
    n_ctx, t_ctx, dm = x_prompt.shape
    n_lat, t_lat, _ = x_sample.shape
    depth = w_in.shape[0]
    grid_w = 64

    cond = jnp.concatenate([c_ctx[None, :], c, jnp.zeros((8 - 1 - n_lat, dm), F32)], axis=0)
    mods = _modulation(cond, ada_w, ada_b.reshape(depth, 1, -1))
    lb_all = jnp.cumsum(jax.nn.softmax(hgrn_lb_logits.astype(F32), axis=1), axis=1)
    lb_all = lb_all - lb_all[:, :1]

    xp = x_prompt.reshape(n_ctx * t_ctx, dm)
    xs = x_sample.reshape(n_lat * t_lat, dm)
    new_rwkv, new_lru, new_hgrn = [], [], []
    yp = ys = None
    for l in range(depth):
        wts = {
            'norm_mix_g': norm_mix_g[l],
            'w_in': w_in[l].astype(BF16),
            'rwkv': {'w0': rwkv_w0[l], 'w_up_pad': _pad_dir_rows(rwkv_w_up[l]).astype(BF16),
                     'a0': rwkv_a0[l], 'a_up_pad': _pad_dir_rows(rwkv_a_up[l]).astype(BF16),
                     'g_up': rwkv_g_up[l].astype(BF16), 'k_k': rwkv_k_k[l], 'k_a': rwkv_k_a[l],
                     'r_k': rwkv_r_k[l], 'ln_w': rwkv_ln_w[l], 'ln_b': rwkv_ln_b[l]},
            'lru': {'conv_w': lru_conv_w[l], 'conv_b': lru_conv_b[l],
                    'wa_bd': _block_diag(lru_wa[l]).astype(BF16), 'ba': lru_ba[l],
                    'wx_bd': _block_diag(lru_wx[l]).astype(BF16), 'bx': lru_bx[l], 'lam': lru_lambda[l]},
            'hgrn_lb': lb_all[:, l], 'hgrn_norm_g': hgrn_norm_g[l],
            'ffn': {'norm_g': norm_ffn_g[l], 'w_out': w_out[l].astype(BF16),
                    'w_gate': ffn_w_gate[l].astype(BF16), 'w_up': ffn_w_up[l].astype(BF16),
                    'w_down': ffn_w_down[l].astype(BF16), 'conv_w': ffn_conv_w[l], 'conv_b': ffn_conv_b[l]},
        }
        last = l == depth - 1
        fg = final_g if last else None
        xp, yp, st = _trunk_layer(xp, mods[l, 0:1].reshape(1, 1, -1), None, None, None, wts,
                                  batch=n_ctx, seq_len=t_ctx, conv_period=t_ctx, want_final=True, final_g=fg)
        new_rwkv.append(st[0])
        new_lru.append(st[1])
        new_hgrn.append(st[2])
        xs, ys, _ = _trunk_layer(xs, mods[l, 1:1 + n_lat].reshape(n_lat, 1, -1),
                                 state_rwkv[:, l].astype(F32), state_rglru[:, l].astype(F32),
                                 state_hgrn[:, l].astype(F32), wts,
                                 batch=n_lat, seq_len=t_lat, conv_period=grid_w, want_final=False, final_g=fg)
    return (yp.reshape(n_ctx, t_ctx, dm), ys.reshape(n_lat, t_lat, dm),
            jnp.stack(new_rwkv, axis=1), jnp.stack(new_lru, axis=1), jnp.stack(new_hgrn, axis=1))
```

```python
import functools
import math

import jax
import jax.numpy as jnp
from jax import lax
from jax.experimental import pallas as pl
from jax.experimental.pallas import tpu as pltpu

F32 = jnp.float32
BF16 = jnp.bfloat16

D_MODEL = 1024
HEAD = 64
PAIR = 2 * HEAD
D_A = 512
D_B = 256
D_C = 256
LORA_G = 128
N_MOD = 6
D_FF = 2816
FF_TILE = 256
PA_COLS = 3 * D_A + 4 * 64 + LORA_G
PB_COLS = 2 * D_B
PC_COLS = 5 * D_C
CHUNK = 64
BLOCK_T = 256
PHASE_A_CHUNKS = 4
TOKEN_TILE = 512
CAST_ROWS = 256
RMS_EPS = 1e-6
GN_EPS = 64e-5
RG_C = 8.0
VMEM_LIMIT = 48 * 1024 * 1024


def _dot(a, b):
    return jnp.dot(a.astype(BF16), b.astype(BF16), preferred_element_type=F32)


def _dot_nt(a, b):
    return lax.dot_general(a.astype(BF16), b.astype(BF16), (((1,), (1,)), ((), ())),
                           preferred_element_type=F32)


def _dot_tn(a, b):
    return lax.dot_general(a.astype(BF16), b.astype(BF16), (((0,), (0,)), ((), ())),
                           preferred_element_type=F32)


def _dot_f32(a, b):
    return jnp.dot(a, b, precision=lax.Precision.HIGHEST, preferred_element_type=F32)


def _split3(x):
    hi = x.astype(BF16)
    r1 = x - hi.astype(F32)
    mid = r1.astype(BF16)
    return hi, mid, (r1 - mid.astype(F32)).astype(BF16)


def _dot_left01(m01, x):
    return sum(jnp.dot(m01, t, preferred_element_type=F32) for t in _split3(x))


def _dot_right01(x, m01):
    hi = x.astype(BF16)
    lo = (x - hi.astype(F32)).astype(BF16)
    return jnp.dot(hi, m01, preferred_element_type=F32) + jnp.dot(lo, m01, preferred_element_type=F32)


def _sigmoid(x):
    return 1.0 / (1.0 + jnp.exp(-x))


def _silu(x):
    return x * _sigmoid(x)


def _softplus(x):
    return jnp.maximum(x, 0.0) + jnp.log1p(jnp.exp(-jnp.abs(x)))


def _iota2(shape, axis):
    return lax.broadcasted_iota(jnp.int32, shape, axis)


def _head_sum_matrix(n=PAIR):
    return ((_iota2((n, n), 0) >> 6) == (_iota2((n, n), 1) >> 6)).astype(BF16)


def _head_sums(x, ind):
    n = ind.shape[0]
    return jnp.concatenate([_dot_right01(x[:, i:i + n], ind) for i in range(0, x.shape[1], n)], axis=1)


def _cumsum_matrix(rev):
    t, s = _iota2((CHUNK, CHUNK), 0), _iota2((CHUNK, CHUNK), 1)
    return ((s >= t) if rev else (s <= t)).astype(BF16)


def _stack_heads(x):
    first = _iota2(x.shape, 1) < HEAD
    return jnp.concatenate([jnp.where(first, x, 0.0), jnp.where(first, 0.0, x)], axis=0)


def _fold_heads(x):
    return x[:CHUNK] + x[CHUNK:]


def _pair_masks(rev):
    i, j = _iota2((PAIR, PAIR), 0), _iota2((PAIR, PAIR), 1)
    same = (i >> 6) == (j >> 6)
    t, s = i & (CHUNK - 1), j & (CHUNK - 1)
    strict = same & ((s > t) if rev else (s < t))
    incl = same & ((s >= t) if rev else (s <= t))
    return same, strict, incl, t, s


def _load_state_pairs(s0_ref, s_scr, n_pairs, transpose):
    z = jnp.zeros((HEAD, HEAD), F32)
    for p in range(n_pairs):
        a, b = s0_ref[2 * p], s0_ref[2 * p + 1]
        if transpose:
            a, b = a.T, b.T
        s_scr[p] = jnp.concatenate([jnp.concatenate([a, z], axis=1), jnp.concatenate([z, b], axis=1)], axis=0)


def _store_state_pairs(sfin_ref, s_scr, n_pairs, transpose):
    for p in range(n_pairs):
        s = s_scr[p]
        a, b = s[:HEAD, :HEAD], s[HEAD:, HEAD:]
        if transpose:
            a, b = a.T, b.T
        sfin_ref[2 * p] = a
        sfin_ref[2 * p + 1] = b


def _cast_kernel(w_ref, o_ref):
    o_ref[...] = w_ref[...].astype(o_ref.dtype)


def _to_bf16(w):
    n_layer, k, n = w.shape
    rows = CAST_ROWS if k % CAST_ROWS == 0 else k
    spec = pl.BlockSpec((None, rows, n), lambda l, i: (l, i, 0))
    return pl.pallas_call(
        _cast_kernel, grid=(n_layer, k // rows), in_specs=[spec], out_specs=spec,
        out_shape=jax.ShapeDtypeStruct(w.shape, BF16), name="cast_bf16",
    )(w)


def _mod_kernel(c_ref, w_ref, b_ref, o_ref):
    c, w = _silu(c_ref[...]), w_ref[...]
    c_hi, w_hi = c.astype(BF16), w.astype(BF16)
    c_lo, w_lo = (c - c_hi.astype(F32)).astype(BF16), (w - w_hi.astype(F32)).astype(BF16)
    d = lambda a, b: jnp.dot(a, b, preferred_element_type=F32)
    o_ref[...] = d(c_hi, w_hi) + (d(c_hi, w_lo) + d(c_lo, w_hi)) + b_ref[...]


def _modulation(cond, ada_w, ada_b):
    n_layer, _, n_out = ada_w.shape
    tn = 512
    return pl.pallas_call(
        _mod_kernel,
        grid=(n_layer, n_out // tn),
        in_specs=[pl.BlockSpec((8, D_MODEL), lambda l, j: (0, 0)),
                  pl.BlockSpec((None, D_MODEL, tn), lambda l, j: (l, 0, j)),
                  pl.BlockSpec((None, 1, tn), lambda l, j: (l, 0, j))],
        out_specs=pl.BlockSpec((None, 8, tn), lambda l, j: (l, 0, j)),
        out_shape=jax.ShapeDtypeStruct((n_layer, 8, n_out), F32),
        name="modulation",
    )(cond, ada_w, ada_b)


def _rms(x):
    return x * lax.rsqrt(jnp.mean(x * x, axis=-1, keepdims=True) + RMS_EPS)


def _norm_in_kernel(x_ref, mod_ref, g_ref, w_ref, pa_ref, pb_ref, pc_ref):
    h = _rms(x_ref[...]) * g_ref[...]
    h = (h * (1.0 + mod_ref[:, D_MODEL:2 * D_MODEL]) + mod_ref[:, 0:D_MODEL]).astype(BF16)
    pa_ref[...] = jnp.dot(h, w_ref[:, 0:PA_COLS], preferred_element_type=F32)
    pb_ref[...] = jnp.dot(h, w_ref[:, PA_COLS:PA_COLS + PB_COLS], preferred_element_type=F32)
    pc_ref[...] = jnp.dot(h, w_ref[:, PA_COLS + PB_COLS:], preferred_element_type=F32)


def _layer_resident(w, layer):
    return pl.BlockSpec((None,) + w.shape[1:], lambda i: (layer, 0, 0), pipeline_mode=pl.Buffered(1))


def _norm_in(x, mods, g, w_in, layer, seq_len):
    n = x.shape[0]
    tiles_per_seq = seq_len // TOKEN_TILE if mods.shape[0] > 1 else None
    mod_map = (lambda i: (i // tiles_per_seq, 0, 0)) if tiles_per_seq else (lambda i: (0, 0, 0))
    const = lambda i: (0, 0)
    return pl.pallas_call(
        _norm_in_kernel,
        grid=(n // TOKEN_TILE,),
        in_specs=[pl.BlockSpec((TOKEN_TILE, D_MODEL), lambda i: (i, 0)),
                  pl.BlockSpec((None, 1, N_MOD * D_MODEL), mod_map),
                  pl.BlockSpec((1, D_MODEL), const),
                  _layer_resident(w_in, layer)],
        out_specs=[pl.BlockSpec((TOKEN_TILE, PA_COLS), lambda i: (i, 0)),
                   pl.BlockSpec((TOKEN_TILE, PB_COLS), lambda i: (i, 0)),
                   pl.BlockSpec((TOKEN_TILE, PC_COLS), lambda i: (i, 0))],
        out_shape=[jax.ShapeDtypeStruct((n, PA_COLS), F32),
                   jax.ShapeDtypeStruct((n, PB_COLS), F32),
                   jax.ShapeDtypeStruct((n, PC_COLS), F32)],
        compiler_params=pltpu.CompilerParams(vmem_limit_bytes=VMEM_LIMIT),
        name="norm_in",
    )(x, mods, g, w_in)


def _rwkv_kernel(*refs, rev, has_init, want_final, post, n_blk):
    refs = list(refs)
    pa_ref = refs.pop(0)
    of_ref = refs.pop(0) if post else None
    s0_ref = refs.pop(0) if has_init else None
    w0_ref, wup_ref, a0_ref, aup_ref, kkw_ref, kaw_ref = (refs.pop(0) for _ in range(6))
    if post:
        gup_ref, rk_ref, lnw_ref, lnb_ref = (refs.pop(0) for _ in range(4))
    o_ref = refs.pop(0)
    sfin_ref = refs.pop(0) if want_final else None
    s_scr, kd_scr, kk_scr, beta_scr, logw_scr, q_scr, g_scr, h_scr, o0_scr, dec_scr = (
        refs.pop(0) for _ in range(10))
    ob_scr = refs.pop(0) if post else None

    n_pairs = D_A // PAIR
    j = pl.program_id(1)
    ind = _head_sum_matrix(2 * PAIR)

    @pl.when(j == 0)
    def _():
        if has_init:
            _load_state_pairs(s0_ref, s_scr, n_pairs, transpose=False)
        else:
            s_scr[...] = jnp.zeros(s_scr.shape, F32)

    k = pa_ref[:, D_A:2 * D_A]
    wl = w0_ref[...] + _dot(jnp.tanh(pa_ref[:, 3 * D_A:3 * D_A + PAIR]), wup_ref[...])
    logw_scr[...] = -_sigmoid(wl) * math.exp(-0.5)
    a = _sigmoid(a0_ref[...] + _dot(pa_ref[:, 3 * D_A + PAIR:3 * D_A + 2 * PAIR], aup_ref[...]))
    kd_scr[...] = k * (1.0 + (a - 1.0) * kaw_ref[...])
    kk = k * kkw_ref[...]
    kk = kk * lax.rsqrt(_head_sums(kk * kk, ind) + 1e-12)
    kk_scr[...] = kk
    beta_scr[...] = kk * a

    tri = _cumsum_matrix(rev)
    same_head = _pair_masks(rev)[0]
    row_t, lane_s = _iota2((CHUNK, PAIR), 0), _iota2((CHUNK, PAIR), 1) & (CHUNK - 1)
    strict_w = (lane_s > row_t) if rev else (lane_s < row_t)
    incl_w = (lane_s >= row_t) if rev else (lane_s <= row_t)
    ident_w = (lane_s == row_t).astype(F32)
    out_buf = ob_scr if post else o_ref
    n_chunks = BLOCK_T // CHUNK

    for c0 in range(0, n_chunks, PHASE_A_CHUNKS):
        items = []
        for c in range(c0, c0 + PHASE_A_CHUNKS):
            rows = slice(c * CHUNK, (c + 1) * CHUNK)
            logw = logw_scr[rows, :]
            b = _dot_left01(tri, logw)
            b_last = b[0:1, :] if rev else b[CHUNK - 1:CHUNK, :]
            p_inv = jnp.exp(-b)
            p_end = jnp.exp(b_last - b)
            r_t = pa_ref[rows, 0:D_A] * jnp.exp(b)
            k_t = kk_scr[rows, :] * jnp.exp(b - logw)
            kd = kd_scr[rows, :]
            beta = beta_scr[rows, :]
            k_h, b_h = kd * p_inv, beta * p_inv
            k_e, b_e = kd * p_end, beta * p_end
            v = pa_ref[rows, 2 * D_A:3 * D_A]
            dec_scr[c * 8:c * 8 + 1, :] = jnp.exp(b_last)
            for p in range(n_pairs):
                ln = slice(p * PAIR, (p + 1) * PAIR)
                items.append(dict(
                    idx=c * n_pairs + p, k_t=k_t[:, ln].astype(BF16), r_t=r_t[:, ln],
                    k_h=k_h[:, ln].astype(BF16), b_h=b_h[:, ln].astype(BF16), v=v[:, ln].astype(BF16),
                    k_e=k_e[:, ln].astype(BF16), b_e=b_e[:, ln].astype(BF16)))
        for it in items:
            lhs = jnp.concatenate([it['k_t'], it['r_t'].astype(BF16)], axis=0)
            rhs = jnp.concatenate([_stack_heads(it.pop('k_h')), _stack_heads(it.pop('b_h'))], axis=0)
            sc = lax.dot_general(lhs, rhs, (((1,), (1,)), ((), ())), preferred_element_type=F32)
            it['a_kr'] = jnp.concatenate([jnp.where(strict_w, sc[:CHUNK, :PAIR], 0.0),
                                          jnp.where(incl_w, sc[CHUNK:, :PAIR], 0.0)], axis=0).astype(BF16)
            it['a_rb'] = jnp.where(incl_w, sc[CHUNK:, PAIR:], 0.0).astype(BF16)
            it['pw'] = jnp.where(strict_w, -sc[:CHUNK, PAIR:], 0.0)
            it['t'] = ident_w + it['pw']
        for it in items:
            pw16 = it['pw'].astype(BF16)
            it['pw'] = jnp.dot(pw16, _stack_heads(pw16), preferred_element_type=F32).astype(BF16)
        for _ in range(4):
            for it in items:
                m = jnp.dot(jnp.concatenate([it['pw'], it['t'].astype(BF16)], axis=0), _stack_heads(it['pw']),
                            preferred_element_type=F32)
                it['pw'] = m[:CHUNK].astype(BF16)
                it['t'] = it['t'] + m[CHUNK:]
        for it in items:
            it['t'] = (it['t'] + jnp.dot(it['t'].astype(BF16), _stack_heads(it.pop('pw')),
                                         preferred_element_type=F32)).astype(BF16)
            it['av'] = jnp.dot(it.pop('a_kr'), _stack_heads(it['v']), preferred_element_type=F32)
        for it in items:
            y0 = it['av'][:CHUNK].astype(BF16)
            rhs = jnp.concatenate([_stack_heads(it.pop('k_t')), _stack_heads(y0)], axis=1)
            it['wu'] = jnp.dot(it.pop('t'), rhs, preferred_element_type=F32).astype(BF16)
        for it in items:
            i = it['idx']
            wu = it.pop('wu')
            w16, u16 = wu[:, :PAIR], wu[:, PAIR:]
            aw = jnp.dot(it.pop('a_rb'), jnp.concatenate([_stack_heads(w16), _stack_heads(u16)], axis=1),
                         preferred_element_type=F32)
            q_scr[i] = (it.pop('r_t') - aw[:, :PAIR]).astype(BF16)
            o0_scr[i] = it.pop('av')[CHUNK:] - aw[:, PAIR:]
            b_e = it.pop('b_e')
            g = lax.dot_general(w16, b_e, (((0,), (0,)), ((), ())), preferred_element_type=F32)
            g_scr[i] = jnp.where(same_head, g, 0.0).astype(BF16)
            h = lax.dot_general(jnp.concatenate([it.pop('v'), u16], axis=0),
                                jnp.concatenate([it.pop('k_e'), -b_e], axis=0),
                                (((0,), (0,)), ((), ())), preferred_element_type=F32)
            h_scr[i] = jnp.where(same_head, h, 0.0)

    for c in range(n_chunks):
        cc = (n_chunks - 1 - c) if rev else c
        rows = slice(cc * CHUNK, (cc + 1) * CHUNK)
        for p in range(n_pairs):
            i = cc * n_pairs + p
            ln = slice(p * PAIR, (p + 1) * PAIR)
            s_bd = s_scr[p]
            s16 = s_bd.astype(BF16)
            o = lax.dot_general(q_scr[i], s16, (((1,), (1,)), ((), ())), preferred_element_type=F32)
            out_buf[rows, ln] = o + o0_scr[i]
            s_scr[p] = (s_bd * dec_scr[cc * 8:cc * 8 + 1, ln]
                        - jnp.dot(s16, g_scr[i], preferred_element_type=F32) + h_scr[i])

    if post:
        r = pa_ref[:, 0:D_A]
        o = of_ref[...] + ob_scr[...]
        mu = _head_sums(o, ind) * (1.0 / HEAD)
        d = o - mu
        var = _head_sums(d * d, ind) * (1.0 / HEAD)
        o = d * lax.rsqrt(var + GN_EPS) * lnw_ref[...] + lnb_ref[...]
        bonus = _head_sums(r * pa_ref[:, D_A:2 * D_A] * rk_ref[...], ind) * pa_ref[:, 2 * D_A:3 * D_A]
        gate = _dot(_sigmoid(pa_ref[:, 3 * D_A + 2 * PAIR:3 * D_A + 2 * PAIR + LORA_G]), gup_ref[...])
        o_ref[...] = ((o + bonus) * gate).astype(o_ref.dtype)

    if want_final:
        @pl.when(j == n_blk - 1)
        def _():
            _store_state_pairs(sfin_ref, s_scr, n_pairs, transpose=False)


def _blk_map(n_blk, rev):
    if rev:
        return lambda b, j: (b * n_blk + n_blk - 1 - j, 0)
    return lambda b, j: (b * n_blk + j, 0)


def _row(v):
    return v.reshape(1, -1)


def _rwkv_dir(pa, o_fwd, s0, wts, *, batch, seq_len, rev, want_final):
    n = pa.shape[0]
    n_blk = seq_len // BLOCK_T
    post = rev
    has_init = s0 is not None
    blk = _blk_map(n_blk, rev)
    const = lambda b, j: (0, 0)
    state_spec = pl.BlockSpec((None, D_A // HEAD, HEAD, HEAD), lambda b, j: (b, 0, 0, 0))
    d = 1 if rev else 0
    ins, specs = [pa], [pl.BlockSpec((BLOCK_T, PA_COLS), blk)]
    if post:
        ins.append(o_fwd)
        specs.append(pl.BlockSpec((BLOCK_T, D_A), blk))
    if has_init:
        ins.append(s0)
        specs.append(state_spec)
    small = [_row(wts['w0'][d]), wts['w_up_pad'][d], _row(wts['a0'][d]), wts['a_up_pad'][d],
             _row(wts['k_k']), _row(wts['k_a'])]
    if post:
        small += [wts['g_up'], _row(wts['r_k']), _row(wts['ln_w']), _row(wts['ln_b'])]
    for wgt in small:
        ins.append(wgt)
        specs.append(pl.BlockSpec(wgt.shape, const))
    out_shape = [jax.ShapeDtypeStruct((n, D_A), BF16 if post else F32)]
    out_specs = [pl.BlockSpec((BLOCK_T, D_A), blk)]
    if want_final:
        out_shape.append(jax.ShapeDtypeStruct((batch, D_A // HEAD, HEAD, HEAD), F32))
        out_specs.append(state_spec)
    n_items = (BLOCK_T // CHUNK) * (D_A // PAIR)
    scratch = [pltpu.VMEM((D_A // PAIR, PAIR, PAIR), F32)] + [pltpu.VMEM((BLOCK_T, D_A), F32)] * 4
    scratch += [pltpu.VMEM((n_items, CHUNK, PAIR), BF16), pltpu.VMEM((n_items, PAIR, PAIR), BF16),
                pltpu.VMEM((n_items, PAIR, PAIR), F32), pltpu.VMEM((n_items, CHUNK, PAIR), F32),
                pltpu.VMEM((8 * BLOCK_T // CHUNK, D_A), F32)]
    if post:
        scratch.append(pltpu.VMEM((BLOCK_T, D_A), F32))
    res = pl.pallas_call(
        functools.partial(_rwkv_kernel, rev=rev, has_init=has_init, want_final=want_final, post=post,
                          n_blk=n_blk),
        grid=(batch, n_blk),
        in_specs=specs, out_specs=out_specs, out_shape=out_shape, scratch_shapes=scratch,
        compiler_params=pltpu.CompilerParams(vmem_limit_bytes=VMEM_LIMIT),
        name="rwkv_bwd" if rev else "rwkv_fwd",
    )(*ins)
    return res[0], (res[1] if want_final else None)


DIAG = 4
HGRN_LEVELS = (4, 8, 16, 32)


def _block_edges(b, rev):
    t = _iota2(b.shape, 0)
    e = b
    out = {}
    for kbit in (1, 2, 4, 8, 16):
        if rev:
            e = jnp.where((t & kbit) != 0, pltpu.roll(e, kbit, 0), e)
        else:
            e = jnp.where((t & kbit) == 0, pltpu.roll(e, CHUNK - kbit, 0), e)
        if 2 * kbit in HGRN_LEVELS:
            out[2 * kbit] = e
    return out


def _hgrn_kernel(*refs, rev, has_init, want_final, post, n_blk):
    refs = list(refs)
    pc_ref = refs.pop(0)
    of_ref = refs.pop(0) if post else None
    s0_ref = refs.pop(0) if has_init else None
    lb_ref = refs.pop(0)
    ng_ref = refs.pop(0) if post else None
    o_ref = refs.pop(0)
    sfin_ref = refs.pop(0) if want_final else None
    s_scr, q_scr, logf_scr, kf_scr, f_scr, qe_scr, ds_scr, o0_scr, dec_scr = (refs.pop(0) for _ in range(9))
    ob_scr = refs.pop(0) if post else None

    n_pairs = D_C // PAIR
    j = pl.program_id(1)
    ind = _head_sum_matrix()

    @pl.when(j == 0)
    def _():
        if has_init:
            _load_state_pairs(s0_ref, s_scr, n_pairs, transpose=True)
        else:
            s_scr[...] = jnp.zeros(s_scr.shape, F32)

    q_scr[...] = _silu(pc_ref[:, 0:D_C])
    f_col = 2 * D_C if rev else D_C
    lb = lb_ref[...]
    f = lb + (1.0 - lb) * _sigmoid(pc_ref[:, f_col:f_col + D_C])
    f_scr[...] = f
    logf_scr[...] = jnp.log(f)
    kf_scr[...] = 1.0 - f

    tri = _cumsum_matrix(rev)
    same_head = _pair_masks(rev)[0]
    t_row, s_lane = _iota2((CHUNK, PAIR), 0), _iota2((CHUNK, PAIR), 1) & (CHUNK - 1)
    level_mask = {}
    for h in HGRN_LEVELS:
        sh = h.bit_length() - 1
        step = -1 if rev else 1
        level_mask[h] = ((t_row >> sh) == (s_lane >> sh) + step) & ((t_row >> (sh + 1)) == (s_lane >> (sh + 1)))
    out_buf = ob_scr if post else o_ref

    n_chunks = BLOCK_T // CHUNK

    for c0 in range(0, n_chunks, PHASE_A_CHUNKS):
        items = []
        for c in range(c0, c0 + PHASE_A_CHUNKS):
            rows = slice(c * CHUNK, (c + 1) * CHUNK)
            b_all = _dot_left01(tri, logf_scr[rows, :])
            for p in range(n_pairs):
                ln = slice(p * PAIR, (p + 1) * PAIR)
                b = b_all[:, ln]
                items.append(dict(
                    idx=c * n_pairs + p, b=b, b_last=b[0:1, :] if rev else b[CHUNK - 1:CHUNK, :],
                    q=q_scr[rows, ln], kf=kf_scr[rows, ln], f=f_scr[rows, ln],
                    v=pc_ref[rows, 3 * D_C + p * PAIR:3 * D_C + (p + 1) * PAIR]))
        for it in items:
            b, q, kf = it['b'], it['q'], it['kf']
            edges = _block_edges(b, rev)
            score = jnp.zeros((CHUNK, PAIR), F32)
            for h in HGRN_LEVELS:
                own = edges[h]
                partner = pltpu.roll(own, (CHUNK - h) if rev else h, 0)
                kp = (kf * jnp.exp(jnp.minimum(own - b, 0.0))).astype(BF16)
                qp = q * jnp.exp(jnp.minimum(b - partner, 0.0))
                sc = _dot_nt(qp, _stack_heads(kp))
                score = jnp.where(level_mask[h], sc, score)
            it['score'] = score.astype(BF16)
        for it in items:
            i = it['idx']
            v16 = it['v'].astype(BF16)
            qe_scr[i] = (it['q'] * jnp.exp(it['b'])).astype(BF16)
            k_e = (it['kf'] * jnp.exp(it['b_last'] - it['b'])).astype(BF16)
            ds = lax.dot_general(v16, k_e, (((0,), (0,)), ((), ())), preferred_element_type=F32)
            ds_scr[i] = jnp.where(same_head, ds, 0.0)
            dec_scr[i * 8:i * 8 + 1, :] = jnp.exp(it['b_last'])
            it['o'] = jnp.dot(it.pop('score'), _stack_heads(v16), preferred_element_type=F32)
        for delta in range(DIAG):
            for it in items:
                q, kf, f, v = it['q'], it['kf'], it['f'], it['v']
                if delta == 0:
                    x, v_s = q * kf, v
                else:
                    roll = lambda z, d: z if d == 0 else pltpu.roll(z, (CHUNK - d) if rev else d, 0)
                    it['e'] = f if delta == 1 else it['e'] * roll(f, delta - 1)
                    x = q * it['e'] * roll(kf, delta)
                    ok = ((t_row & (DIAG - 1)) <= DIAG - 1 - delta) if rev else ((t_row & (DIAG - 1)) >= delta)
                    x = jnp.where(ok, x, 0.0)
                    v_s = roll(v, delta)
                it['o'] = it['o'] + _dot(x, ind) * v_s
        for it in items:
            o0_scr[it['idx']] = it['o']

    for c in range(n_chunks):
        cc = (n_chunks - 1 - c) if rev else c
        rows = slice(cc * CHUNK, (cc + 1) * CHUNK)
        for p in range(n_pairs):
            i = cc * n_pairs + p
            s_bd = s_scr[p]
            o = lax.dot_general(qe_scr[i], s_bd.astype(BF16), (((1,), (1,)), ((), ())),
                                preferred_element_type=F32)
            out_buf[rows, p * PAIR:(p + 1) * PAIR] = o + o0_scr[i]
            s_scr[p] = s_bd * dec_scr[i * 8:i * 8 + 1, :] + ds_scr[i]

    if post:
        o = of_ref[...] + ob_scr[...]
        ms = _head_sums(o * o, _head_sum_matrix(D_C)) * (1.0 / HEAD)
        o = o * lax.rsqrt(ms + RMS_EPS) * ng_ref[...] * _silu(pc_ref[:, 4 * D_C:5 * D_C])
        o_ref[...] = o.astype(o_ref.dtype)

    if want_final:
        @pl.when(j == n_blk - 1)
        def _():
            _store_state_pairs(sfin_ref, s_scr, n_pairs, transpose=True)


def _hgrn_dir(pc, o_fwd, s0, lb, norm_g, *, batch, seq_len, rev, want_final):
    n = pc.shape[0]
    n_blk = seq_len // BLOCK_T
    post = rev
    has_init = s0 is not None
    blk = _blk_map(n_blk, rev)
    const = lambda b, j: (0, 0)
    state_spec = pl.BlockSpec((None, D_C // HEAD, HEAD, HEAD), lambda b, j: (b, 0, 0, 0))
    ins, specs = [pc], [pl.BlockSpec((BLOCK_T, PC_COLS), blk)]
    if post:
        ins.append(o_fwd)
        specs.append(pl.BlockSpec((BLOCK_T, D_C), blk))
    if has_init:
        ins.append(s0)
        specs.append(state_spec)
    ins.append(_row(lb))
    specs.append(pl.BlockSpec((1, D_C), const))
    if post:
        ins.append(_row(norm_g))
        specs.append(pl.BlockSpec((1, D_C), const))
    out_shape = [jax.ShapeDtypeStruct((n, D_C), BF16 if post else F32)]
    out_specs = [pl.BlockSpec((BLOCK_T, D_C), blk)]
    if want_final:
        out_shape.append(jax.ShapeDtypeStruct((batch, D_C // HEAD, HEAD, HEAD), F32))
        out_specs.append(state_spec)
    n_items = (BLOCK_T // CHUNK) * (D_C // PAIR)
    scratch = [pltpu.VMEM((D_C // PAIR, PAIR, PAIR), F32)] + [pltpu.VMEM((BLOCK_T, D_C), F32)] * 4
    scratch += [pltpu.VMEM((n_items, CHUNK, PAIR), BF16), pltpu.VMEM((n_items, PAIR, PAIR), F32),
                pltpu.VMEM((n_items, CHUNK, PAIR), F32), pltpu.VMEM((8 * n_items, PAIR), F32)]
    if post:
        scratch.append(pltpu.VMEM((BLOCK_T, D_C), F32))
    res = pl.pallas_call(
        functools.partial(_hgrn_kernel, rev=rev, has_init=has_init, want_final=want_final, post=post,
                          n_blk=n_blk),
        grid=(batch, n_blk),
        in_specs=specs, out_specs=out_specs, out_shape=out_shape, scratch_shapes=scratch,
        compiler_params=pltpu.CompilerParams(vmem_limit_bytes=VMEM_LIMIT),
        name="hgrn_bwd" if rev else "hgrn_fwd",
    )(*ins)
    return res[0], (res[1] if want_final else None)


def _shift_rows(x, s, t_idx):
    n = x.shape[0]
    y = pltpu.roll(x, s % n, 0)
    return jnp.where((t_idx >= s) if s > 0 else (t_idx < n + s), y, 0.0)


SCAN_GROUP = 8


def _group_scan(a, b, t_idx, rev):
    n = a.shape[0]
    in_group = t_idx & (SCAN_GROUP - 1)
    k = 1
    while k < SCAN_GROUP:
        ok = (in_group < SCAN_GROUP - k) if rev else (in_group >= k)
        sh = (n - k) if rev else k
        b = jnp.where(ok, a * pltpu.roll(b, sh, 0) + b, b)
        a = jnp.where(ok, a * pltpu.roll(a, sh, 0), a)
        k *= 2
    return a, b


def _gelu_tanh(x):
    return 0.5 * x * (1.0 + jnp.tanh(math.sqrt(2.0 / math.pi) * (x + 0.044715 * x * x * x)))


def _lru_kernel(*refs, has_init, want_final):
    refs = list(refs)
    pb_ref = refs.pop(0)
    h0_ref = refs.pop(0) if has_init else None
    cw_ref, cb_ref, wa_ref, ba_ref, wx_ref, bx_ref, lam_ref = (refs.pop(0) for _ in range(7))
    y_ref = refs.pop(0)
    hfin_ref = refs.pop(0) if want_final else None
    a_scr, b_scr, h_scr = (refs.pop(0) for _ in range(3))

    n = pb_ref.shape[0]
    t_idx = _iota2((n, D_B), 0)
    xb = pb_ref[:, 0:D_B]
    u = cb_ref[...] + cw_ref[2:3, :] * xb
    u = u + cw_ref[0:1, :] * _shift_rows(xb, 2, t_idx)
    u = u + cw_ref[1:2, :] * _shift_rows(xb, 1, t_idx)
    u = u + cw_ref[3:4, :] * _shift_rows(xb, -1, t_idx)
    for d in range(2):
        rg = _sigmoid(_dot(u, wa_ref[d]) + ba_ref[d:d + 1, :])
        ig = _sigmoid(_dot(u, wx_ref[d]) + bx_ref[d:d + 1, :])
        log_a = -RG_C * rg * _softplus(-lam_ref[d:d + 1, :])
        a_t = jnp.exp(log_a)
        in_scale = jnp.sqrt((1.0 + a_t * a_t) * jnp.tanh(-log_a))
        a_scr[d], b_scr[d] = _group_scan(a_t, in_scale * (ig * u), t_idx, rev=d == 1)

    n_groups = n // SCAN_GROUP

    def carry_body(g, carry):
        h_f, h_b = carry
        rows_f = pl.ds(pl.multiple_of(g * SCAN_GROUP, SCAN_GROUP), SCAN_GROUP)
        rows_b = pl.ds(pl.multiple_of((n_groups - 1 - g) * SCAN_GROUP, SCAN_GROUP), SCAN_GROUP)
        blk_f = b_scr[0, rows_f, :] + a_scr[0, rows_f, :] * h_f
        blk_b = b_scr[1, rows_b, :] + a_scr[1, rows_b, :] * h_b
        h_scr[0, rows_f, :] = blk_f
        h_scr[1, rows_b, :] = blk_b
        return blk_f[SCAN_GROUP - 1:SCAN_GROUP, :], blk_b[0:1, :]

    if has_init:
        init = (h0_ref[0:1, :], h0_ref[1:2, :])
    else:
        init = (jnp.zeros((1, D_B), F32), jnp.zeros((1, D_B), F32))
    h_f, h_b = lax.fori_loop(0, n_groups, carry_body, init, unroll=4)
    if want_final:
        hfin_ref[0:1, :] = h_f
        hfin_ref[1:2, :] = h_b
    y_ref[...] = ((h_scr[0] + h_scr[1]) * _gelu_tanh(pb_ref[:, D_B:2 * D_B])).astype(y_ref.dtype)


def _lru(pb, h0, wts, *, batch, seq_len, want_final):
    n = pb.shape[0]
    has_init = h0 is not None
    const2 = lambda b: (0, 0)
    const3 = lambda b: (0, 0, 0)
    ins, specs = [pb], [pl.BlockSpec((seq_len, PB_COLS), lambda b: (b, 0))]
    state_spec = pl.BlockSpec((None, 2, D_B), lambda b: (b, 0, 0))
    if has_init:
        ins.append(h0)
        specs.append(state_spec)
    for wgt in (wts['conv_w'], _row(wts['conv_b']), wts['wa_bd'], wts['ba'], wts['wx_bd'], wts['bx'],
                wts['lam']):
        ins.append(wgt)
        specs.append(pl.BlockSpec(wgt.shape, const3 if wgt.ndim == 3 else const2))
    out_shape = [jax.ShapeDtypeStruct((n, D_B), BF16)]
    out_specs = [pl.BlockSpec((seq_len, D_B), lambda b: (b, 0))]
    if want_final:
        out_shape.append(jax.ShapeDtypeStruct((batch, 2, D_B), F32))
        out_specs.append(state_spec)
    res = pl.pallas_call(
        functools.partial(_lru_kernel, has_init=has_init, want_final=want_final),
        grid=(batch,),
        in_specs=specs, out_specs=out_specs, out_shape=out_shape,
        scratch_shapes=[pltpu.VMEM((2, seq_len, D_B), F32)] * 3,
        compiler_params=pltpu.CompilerParams(vmem_limit_bytes=VMEM_LIMIT),
        name="lru",
    )(*ins)
    return res[0], (res[1] if want_final else None)


def _out_ffn_kernel(*refs, conv_period, final):
    refs = list(refs)
    x_ref, ma_ref, mb_ref, mc_ref, mod_ref, g_ref, wo_ref, wg_ref, wu_ref, wd_ref, cw_ref, cb_ref = (
        refs.pop(0) for _ in range(12))
    fg_ref = refs.pop(0) if final else None
    xo_ref = refs.pop(0)
    yo_ref = refs.pop(0) if final else None

    dm = D_MODEL
    mix = jnp.dot(ma_ref[...], wo_ref[0:D_A, :], preferred_element_type=F32)
    mix = mix + jnp.dot(mb_ref[...], wo_ref[D_A:D_A + D_B, :], preferred_element_type=F32)
    mix = mix + jnp.dot(mc_ref[...], wo_ref[D_A + D_B:, :], preferred_element_type=F32)
    x = x_ref[...] + mod_ref[:, 2 * dm:3 * dm] * mix
    h = _rms(x) * g_ref[...]
    h = (h * (1.0 + mod_ref[:, 4 * dm:5 * dm]) + mod_ref[:, 3 * dm:4 * dm]).astype(BF16)
    tm = x.shape[0]
    t_idx = _iota2((tm, FF_TILE), 0) & (conv_period - 1)
    acc = jnp.zeros((tm, dm), F32)
    for f0 in range(0, D_FF, FF_TILE):
        cols = slice(f0, f0 + FF_TILE)
        g = jnp.dot(h, wg_ref[:, cols], preferred_element_type=F32)
        g_prev = jnp.where(t_idx == 0, 0.0, pltpu.roll(g, 1, 0))
        g_next = jnp.where(t_idx == conv_period - 1, 0.0, pltpu.roll(g, tm - 1, 0))
        g = cw_ref[0:1, cols] * g_prev + cw_ref[1:2, cols] * g + cw_ref[2:3, cols] * g_next + cb_ref[:, cols]
        up = jnp.dot(h, wu_ref[:, cols], preferred_element_type=F32)
        acc = acc + jnp.dot((_silu(g) * up).astype(BF16), wd_ref[cols, :], preferred_element_type=F32)
    x = x + mod_ref[:, 5 * dm:6 * dm] * acc
    xo_ref[...] = x
    if final:
        yo_ref[...] = _rms(x) * fg_ref[...]


def _out_ffn(x, mix_a, mix_b, mix_c, mods, wts, *, seq_len, conv_period, final_g):
    n = x.shape[0]
    final = final_g is not None
    tiles_per_seq = seq_len // TOKEN_TILE if mods.shape[0] > 1 else None
    mod_map = (lambda i: (i // tiles_per_seq, 0, 0)) if tiles_per_seq else (lambda i: (0, 0, 0))
    const = lambda i: (0, 0)
    tok = lambda width: pl.BlockSpec((TOKEN_TILE, width), lambda i: (i, 0))
    resident = lambda w: _layer_resident(w, wts['layer'])
    ins = [x, mix_a, mix_b, mix_c, mods, _row(wts['norm_g']), wts['w_out'], wts['w_gate'], wts['w_up'],
           wts['w_down'], wts['conv_w'], _row(wts['conv_b'])]
    specs = [tok(D_MODEL), tok(D_A), tok(D_B), tok(D_C), pl.BlockSpec((None, 1, N_MOD * D_MODEL), mod_map),
             pl.BlockSpec((1, D_MODEL), const), resident(wts['w_out']), resident(wts['w_gate']),
             resident(wts['w_up']), resident(wts['w_down']), pl.BlockSpec(wts['conv_w'].shape, const),
             pl.BlockSpec((1, D_FF), const)]
    out_shape = [jax.ShapeDtypeStruct((n, D_MODEL), F32)]
    out_specs = [tok(D_MODEL)]
    if final:
        ins.append(_row(final_g))
        specs.append(pl.BlockSpec((1, D_MODEL), const))
        out_shape.append(jax.ShapeDtypeStruct((n, D_MODEL), F32))
        out_specs.append(tok(D_MODEL))
    res = pl.pallas_call(
        functools.partial(_out_ffn_kernel, conv_period=conv_period, final=final),
        grid=(n // TOKEN_TILE,),
        in_specs=specs, out_specs=out_specs, out_shape=out_shape,
        compiler_params=pltpu.CompilerParams(vmem_limit_bytes=VMEM_LIMIT),
        name="out_ffn",
    )(*ins)
    return res[0], (res[1] if final else None)


def _trunk_layer(x, mods, s_rwkv, s_lru, s_hgrn, wts, *, batch, seq_len, conv_period, want_final, final_g):
    pa, pb, pc = _norm_in(x, mods, _row(wts['norm_mix_g']), wts['w_in'], wts['layer'], seq_len)
    kw = dict(batch=batch, seq_len=seq_len, want_final=want_final)
    sel = lambda s, d: None if s is None else s[:, d]
    o_f, sa_f = _rwkv_dir(pa, None, sel(s_rwkv, 0), wts['rwkv'], rev=False, **kw)
    mix_a, sa_b = _rwkv_dir(pa, o_f, sel(s_rwkv, 1), wts['rwkv'], rev=True, **kw)
    c_f, sc_f = _hgrn_dir(pc, None, sel(s_hgrn, 0), wts['hgrn_lb'][0], wts['hgrn_norm_g'], rev=False, **kw)
    mix_c, sc_b = _hgrn_dir(pc, c_f, sel(s_hgrn, 1), wts['hgrn_lb'][1], wts['hgrn_norm_g'], rev=True, **kw)
    mix_b, sb = _lru(pb, s_lru, wts['lru'], **kw)
    x, y = _out_ffn(x, mix_a, mix_b, mix_c, mods, wts['ffn'], seq_len=seq_len, conv_period=conv_period,
                    final_g=final_g)
    states = None
    if want_final:
        states = (jnp.stack([sa_f, sa_b], axis=1), sb, jnp.stack([sc_f, sc_b], axis=1))
    return x, y, states


def _block_diag(w):
    nd, nb, bi, bj = w.shape
    eye = jnp.eye(nb, dtype=w.dtype)
    return jnp.einsum('dhij,hg->dhigj', w, eye).reshape(nd, nb * bi, nb * bj)


def _pad_dir_rows(w):
    z = jnp.zeros_like(w[0])
    return jnp.stack([jnp.concatenate([w[0], z], axis=0), jnp.concatenate([z, w[1]], axis=0)], axis=0)


def kernel(x_prompt, x_sample, state_rwkv, state_rglru, state_hgrn, c, c_ctx, ada_w, ada_b, norm_mix_g, norm_ffn_g, w_in, w_out, rwkv_w0, rwkv_w_up, rwkv_a0, rwkv_a_up, rwkv_g_up, rwkv_k_k, rwkv_k_a, rwkv_r_k, rwkv_ln_w, rwkv_ln_b, lru_conv_w, lru_conv_b, lru_wa, lru_ba, lru_wx, lru_bx, lru_lambda, hgrn_lb_logits, hgrn_norm_g, ffn_w_gate, ffn_w_up, ffn_conv_w, ffn_conv_b, ffn_w_down, final_g):
    n_ctx, t_ctx, dm = x_prompt.shape
    n_lat, t_lat, _ = x_sample.shape
    depth = w_in.shape[0]
    grid_w = 64

    cond = jnp.concatenate([c_ctx[None, :], c, jnp.zeros((8 - 1 - n_lat, dm), F32)], axis=0)
    mods = _modulation(cond, ada_w, ada_b.reshape(depth, 1, -1))
    lb_all = jnp.cumsum(jax.nn.softmax(hgrn_lb_logits.astype(F32), axis=1), axis=1)
    lb_all = lb_all - lb_all[:, :1]

    xp = x_prompt.reshape(n_ctx * t_ctx, dm)
    xs = x_sample.reshape(n_lat * t_lat, dm)
    new_rwkv, new_lru, new_hgrn = [], [], []
    yp = ys = None
    w_in16, w_out16 = _to_bf16(w_in), _to_bf16(w_out)
    w_gate16, w_up16, w_down16 = _to_bf16(ffn_w_gate), _to_bf16(ffn_w_up), _to_bf16(ffn_w_down)
    for l in range(depth):
        wts = {
            'norm_mix_g': norm_mix_g[l], 'layer': l,
            'w_in': w_in16,
            'rwkv': {'w0': rwkv_w0[l], 'w_up_pad': _pad_dir_rows(rwkv_w_up[l]).astype(BF16),
                     'a0': rwkv_a0[l], 'a_up_pad': _pad_dir_rows(rwkv_a_up[l]).astype(BF16),
                     'g_up': rwkv_g_up[l].astype(BF16), 'k_k': rwkv_k_k[l], 'k_a': rwkv_k_a[l],
                     'r_k': rwkv_r_k[l], 'ln_w': rwkv_ln_w[l], 'ln_b': rwkv_ln_b[l]},
            'lru': {'conv_w': lru_conv_w[l], 'conv_b': lru_conv_b[l],
                    'wa_bd': _block_diag(lru_wa[l]).astype(BF16), 'ba': lru_ba[l],
                    'wx_bd': _block_diag(lru_wx[l]).astype(BF16), 'bx': lru_bx[l], 'lam': lru_lambda[l]},
            'hgrn_lb': lb_all[:, l], 'hgrn_norm_g': hgrn_norm_g[l],
            'ffn': {'norm_g': norm_ffn_g[l], 'layer': l, 'w_out': w_out16, 'w_gate': w_gate16, 'w_up': w_up16,
                    'w_down': w_down16, 'conv_w': ffn_conv_w[l], 'conv_b': ffn_conv_b[l]},
        }
        last = l == depth - 1
        fg = final_g if last else None
        xp, yp, st = _trunk_layer(xp, mods[l, 0:1].reshape(1, 1, -1), None, None, None, wts,
                                  batch=n_ctx, seq_len=t_ctx, conv_period=t_ctx, want_final=True, final_g=fg)
        new_rwkv.append(st[0])
        new_lru.append(st[1])
        new_hgrn.append(st[2])
        xs, ys, _ = _trunk_layer(xs, mods[l, 1:1 + n_lat].reshape(n_lat, 1, -1),
                                 state_rwkv[:, l].astype(F32), state_rglru[:, l].astype(F32),
                                 state_hgrn[:, l].astype(F32), wts,
                                 batch=n_lat, seq_len=t_lat, conv_period=grid_w, want_final=False, final_g=fg)
    return (yp.reshape(n_ctx, t_ctx, dm), ys.reshape(n_lat, t_lat, dm),
            jnp.stack(new_rwkv, axis=1), jnp.stack(new_lru, axis=1), jnp.stack(new_hgrn, axis=1))
```

```python
import functools
import math

import jax
import jax.numpy as jnp
from jax import lax
from jax.experimental import pallas as pl
from jax.experimental.pallas import tpu as pltpu

F32 = jnp.float32
BF16 = jnp.bfloat16

D_MODEL = 1024
HEAD = 64
PAIR = 2 * HEAD
D_A = 512
D_B = 256
D_C = 256
LORA_G = 128
N_MOD = 6
D_FF = 2816
FF_TILE = 256
PA_COLS = 3 * D_A + 4 * 64 + LORA_G
PB_COLS = 2 * D_B
PC_COLS = 5 * D_C
CHUNK = 64
BLOCK_T = 256
PHASE_A_CHUNKS = 4
TOKEN_TILE = 512
CAST_ROWS = 256
RMS_EPS = 1e-6
GN_EPS = 64e-5
RG_C = 8.0
VMEM_LIMIT = 48 * 1024 * 1024


def _dot(a, b):
    return jnp.dot(a.astype(BF16), b.astype(BF16), preferred_element_type=F32)


def _dot_nt(a, b):
    return lax.dot_general(a.astype(BF16), b.astype(BF16), (((1,), (1,)), ((), ())),
                           preferred_element_type=F32)


def _dot_tn(a, b):
    return lax.dot_general(a.astype(BF16), b.astype(BF16), (((0,), (0,)), ((), ())),
                           preferred_element_type=F32)


def _dot_f32(a, b):
    return jnp.dot(a, b, precision=lax.Precision.HIGHEST, preferred_element_type=F32)


def _split3(x):
    hi = x.astype(BF16)
    r1 = x - hi.astype(F32)
    mid = r1.astype(BF16)
    return hi, mid, (r1 - mid.astype(F32)).astype(BF16)


def _dot_left01(m01, x):
    return sum(jnp.dot(m01, t, preferred_element_type=F32) for t in _split3(x))


def _dot_right01(x, m01):
    hi = x.astype(BF16)
    lo = (x - hi.astype(F32)).astype(BF16)
    return jnp.dot(hi, m01, preferred_element_type=F32) + jnp.dot(lo, m01, preferred_element_type=F32)


def _sigmoid(x):
    return 1.0 / (1.0 + jnp.exp(-x))


def _silu(x):
    return x * _sigmoid(x)


def _softplus(x):
    return jnp.maximum(x, 0.0) + jnp.log1p(jnp.exp(-jnp.abs(x)))


def _iota2(shape, axis):
    return lax.broadcasted_iota(jnp.int32, shape, axis)


def _head_sum_matrix(n=PAIR):
    return ((_iota2((n, n), 0) >> 6) == (_iota2((n, n), 1) >> 6)).astype(BF16)


def _head_sums(x, ind):
    n = ind.shape[0]
    return jnp.concatenate([_dot_right01(x[:, i:i + n], ind) for i in range(0, x.shape[1], n)], axis=1)


def _cumsum_matrix(rev):
    t, s = _iota2((CHUNK, CHUNK), 0), _iota2((CHUNK, CHUNK), 1)
    return ((s >= t) if rev else (s <= t)).astype(BF16)


def _stack_heads(x):
    first = _iota2(x.shape, 1) < HEAD
    return jnp.concatenate([jnp.where(first, x, 0.0), jnp.where(first, 0.0, x)], axis=0)


def _fold_heads(x):
    return x[:CHUNK] + x[CHUNK:]


def _pair_masks(rev):
    i, j = _iota2((PAIR, PAIR), 0), _iota2((PAIR, PAIR), 1)
    same = (i >> 6) == (j >> 6)
    t, s = i & (CHUNK - 1), j & (CHUNK - 1)
    strict = same & ((s > t) if rev else (s < t))
    incl = same & ((s >= t) if rev else (s <= t))
    return same, strict, incl, t, s


def _load_state_pairs(s0_ref, s_scr, n_pairs, transpose):
    z = jnp.zeros((HEAD, HEAD), F32)
    for p in range(n_pairs):
        a, b = s0_ref[2 * p], s0_ref[2 * p + 1]
        if transpose:
            a, b = a.T, b.T
        s_scr[p] = jnp.concatenate([jnp.concatenate([a, z], axis=1), jnp.concatenate([z, b], axis=1)], axis=0)


def _store_state_pairs(sfin_ref, s_scr, n_pairs, transpose):
    for p in range(n_pairs):
        s = s_scr[p]
        a, b = s[:HEAD, :HEAD], s[HEAD:, HEAD:]
        if transpose:
            a, b = a.T, b.T
        sfin_ref[2 * p] = a
        sfin_ref[2 * p + 1] = b


def _cast_kernel(w_ref, o_ref):
    o_ref[...] = w_ref[...].astype(o_ref.dtype)


def _to_bf16(w):
    n_layer, k, n = w.shape
    rows = CAST_ROWS if k % CAST_ROWS == 0 else k
    spec = pl.BlockSpec((None, rows, n), lambda l, i: (l, i, 0))
    return pl.pallas_call(
        _cast_kernel, grid=(n_layer, k // rows), in_specs=[spec], out_specs=spec,
        out_shape=jax.ShapeDtypeStruct(w.shape, BF16), name="cast_bf16",
    )(w)


def _mod_kernel(c_ref, w_ref, b_ref, o_ref):
    c, w = _silu(c_ref[...]), w_ref[...]
    c_hi, w_hi = c.astype(BF16), w.astype(BF16)
    c_lo, w_lo = (c - c_hi.astype(F32)).astype(BF16), (w - w_hi.astype(F32)).astype(BF16)
    d = lambda a, b: jnp.dot(a, b, preferred_element_type=F32)
    o_ref[...] = d(c_hi, w_hi) + (d(c_hi, w_lo) + d(c_lo, w_hi)) + b_ref[...]


def _modulation(cond, ada_w, ada_b):
    n_layer, _, n_out = ada_w.shape
    tn = 512
    return pl.pallas_call(
        _mod_kernel,
        grid=(n_layer, n_out // tn),
        in_specs=[pl.BlockSpec((8, D_MODEL), lambda l, j: (0, 0)),
                  pl.BlockSpec((None, D_MODEL, tn), lambda l, j: (l, 0, j)),
                  pl.BlockSpec((None, 1, tn), lambda l, j: (l, 0, j))],
        out_specs=pl.BlockSpec((None, 8, tn), lambda l, j: (l, 0, j)),
        out_shape=jax.ShapeDtypeStruct((n_layer, 8, n_out), F32),
        name="modulation",
    )(cond, ada_w, ada_b)


def _rms(x):
    return x * lax.rsqrt(jnp.mean(x * x, axis=-1, keepdims=True) + RMS_EPS)


def _norm_in_kernel(x_ref, mod_ref, g_ref, w_ref, pa_ref, pb_ref, pc_ref):
    h = _rms(x_ref[...]) * g_ref[...]
    h = (h * (1.0 + mod_ref[:, D_MODEL:2 * D_MODEL]) + mod_ref[:, 0:D_MODEL]).astype(BF16)
    pa_ref[...] = jnp.dot(h, w_ref[:, 0:PA_COLS], preferred_element_type=F32)
    pb_ref[...] = jnp.dot(h, w_ref[:, PA_COLS:PA_COLS + PB_COLS], preferred_element_type=F32)
    pc_ref[...] = jnp.dot(h, w_ref[:, PA_COLS + PB_COLS:], preferred_element_type=F32)


def _layer_resident(w, layer):
    return pl.BlockSpec((None,) + w.shape[1:], lambda i: (layer, 0, 0), pipeline_mode=pl.Buffered(1))


def _norm_in(x, mods, g, w_in, layer, seq_len):
    n = x.shape[0]
    tiles_per_seq = seq_len // TOKEN_TILE if mods.shape[0] > 1 else None
    mod_map = (lambda i: (i // tiles_per_seq, 0, 0)) if tiles_per_seq else (lambda i: (0, 0, 0))
    const = lambda i: (0, 0)
    return pl.pallas_call(
        _norm_in_kernel,
        grid=(n // TOKEN_TILE,),
        in_specs=[pl.BlockSpec((TOKEN_TILE, D_MODEL), lambda i: (i, 0)),
                  pl.BlockSpec((None, 1, N_MOD * D_MODEL), mod_map),
                  pl.BlockSpec((1, D_MODEL), const),
                  _layer_resident(w_in, layer)],
        out_specs=[pl.BlockSpec((TOKEN_TILE, PA_COLS), lambda i: (i, 0)),
                   pl.BlockSpec((TOKEN_TILE, PB_COLS), lambda i: (i, 0)),
                   pl.BlockSpec((TOKEN_TILE, PC_COLS), lambda i: (i, 0))],
        out_shape=[jax.ShapeDtypeStruct((n, PA_COLS), F32),
                   jax.ShapeDtypeStruct((n, PB_COLS), F32),
                   jax.ShapeDtypeStruct((n, PC_COLS), F32)],
        compiler_params=pltpu.CompilerParams(vmem_limit_bytes=VMEM_LIMIT),
        name="norm_in",
    )(x, mods, g, w_in)


def _rwkv_kernel(*refs, rev, has_init, want_final, post, n_blk):
    refs = list(refs)
    pa_ref = refs.pop(0)
    of_ref = refs.pop(0) if post else None
    s0_ref = refs.pop(0) if has_init else None
    w0_ref, wup_ref, a0_ref, aup_ref, kkw_ref, kaw_ref = (refs.pop(0) for _ in range(6))
    if post:
        gup_ref, rk_ref, lnw_ref, lnb_ref = (refs.pop(0) for _ in range(4))
    o_ref = refs.pop(0)
    sfin_ref = refs.pop(0) if want_final else None
    s_scr, kd_scr, kk_scr, beta_scr, logw_scr, q_scr, g_scr, h_scr, o0_scr, dec_scr = (
        refs.pop(0) for _ in range(10))
    ob_scr = refs.pop(0) if post else None

    n_pairs = D_A // PAIR
    j = pl.program_id(1)
    ind = _head_sum_matrix(2 * PAIR)

    @pl.when(j == 0)
    def _():
        if has_init:
            _load_state_pairs(s0_ref, s_scr, n_pairs, transpose=False)
        else:
            s_scr[...] = jnp.zeros(s_scr.shape, F32)

    k = pa_ref[:, D_A:2 * D_A]
    wl = w0_ref[...] + _dot(jnp.tanh(pa_ref[:, 3 * D_A:3 * D_A + PAIR]), wup_ref[...])
    logw_scr[...] = -_sigmoid(wl) * math.exp(-0.5)
    a = _sigmoid(a0_ref[...] + _dot(pa_ref[:, 3 * D_A + PAIR:3 * D_A + 2 * PAIR], aup_ref[...]))
    kd_scr[...] = k * (1.0 + (a - 1.0) * kaw_ref[...])
    kk = k * kkw_ref[...]
    kk = kk * lax.rsqrt(_head_sums(kk * kk, ind) + 1e-12)
    kk_scr[...] = kk
    beta_scr[...] = kk * a

    tri = _cumsum_matrix(rev)
    same_head = _pair_masks(rev)[0]
    row_t, lane_s = _iota2((CHUNK, PAIR), 0), _iota2((CHUNK, PAIR), 1) & (CHUNK - 1)
    strict_w = (lane_s > row_t) if rev else (lane_s < row_t)
    incl_w = (lane_s >= row_t) if rev else (lane_s <= row_t)
    ident_w = (lane_s == row_t).astype(F32)
    out_buf = ob_scr if post else o_ref
    n_chunks = BLOCK_T // CHUNK

    for c0 in range(0, n_chunks, PHASE_A_CHUNKS):
        items = []
        for c in range(c0, c0 + PHASE_A_CHUNKS):
            rows = slice(c * CHUNK, (c + 1) * CHUNK)
            logw = logw_scr[rows, :]
            b = _dot_left01(tri, logw)
            b_last = b[0:1, :] if rev else b[CHUNK - 1:CHUNK, :]
            p_inv = jnp.exp(-b)
            p_end = jnp.exp(b_last - b)
            r_t = pa_ref[rows, 0:D_A] * jnp.exp(b)
            k_t = kk_scr[rows, :] * jnp.exp(b - logw)
            kd = kd_scr[rows, :]
            beta = beta_scr[rows, :]
            k_h, b_h = kd * p_inv, beta * p_inv
            k_e, b_e = kd * p_end, beta * p_end
            v = pa_ref[rows, 2 * D_A:3 * D_A]
            dec_scr[c * 8:c * 8 + 1, :] = jnp.exp(b_last)
            for p in range(n_pairs):
                ln = slice(p * PAIR, (p + 1) * PAIR)
                items.append(dict(
                    idx=c * n_pairs + p, k_t=k_t[:, ln].astype(BF16), r_t=r_t[:, ln],
                    k_h=k_h[:, ln].astype(BF16), b_h=b_h[:, ln].astype(BF16), v=v[:, ln].astype(BF16),
                    k_e=k_e[:, ln].astype(BF16), b_e=b_e[:, ln].astype(BF16)))
        for it in items:
            lhs = jnp.concatenate([it['k_t'], it['r_t'].astype(BF16)], axis=0)
            rhs = jnp.concatenate([_stack_heads(it.pop('k_h')), _stack_heads(it.pop('b_h'))], axis=0)
            sc = lax.dot_general(lhs, rhs, (((1,), (1,)), ((), ())), preferred_element_type=F32)
            it['a_kr'] = jnp.concatenate([jnp.where(strict_w, sc[:CHUNK, :PAIR], 0.0),
                                          jnp.where(incl_w, sc[CHUNK:, :PAIR], 0.0)], axis=0).astype(BF16)
            it['a_rb'] = jnp.where(incl_w, sc[CHUNK:, PAIR:], 0.0).astype(BF16)
            it['pw'] = jnp.where(strict_w, -sc[:CHUNK, PAIR:], 0.0)
            it['t'] = ident_w + it['pw']
        for it in items:
            pw16 = it['pw'].astype(BF16)
            it['pw'] = jnp.dot(pw16, _stack_heads(pw16), preferred_element_type=F32).astype(BF16)
        for _ in range(4):
            for it in items:
                m = jnp.dot(jnp.concatenate([it['pw'], it['t'].astype(BF16)], axis=0), _stack_heads(it['pw']),
                            preferred_element_type=F32)
                it['pw'] = m[:CHUNK].astype(BF16)
                it['t'] = it['t'] + m[CHUNK:]
        for it in items:
            it['t'] = (it['t'] + jnp.dot(it['t'].astype(BF16), _stack_heads(it.pop('pw')),
                                         preferred_element_type=F32)).astype(BF16)
            it['av'] = jnp.dot(it.pop('a_kr'), _stack_heads(it['v']), preferred_element_type=F32)
        for it in items:
            y0 = it['av'][:CHUNK].astype(BF16)
            rhs = jnp.concatenate([_stack_heads(it.pop('k_t')), _stack_heads(y0)], axis=1)
            it['wu'] = jnp.dot(it.pop('t'), rhs, preferred_element_type=F32).astype(BF16)
        for it in items:
            i = it['idx']
            wu = it.pop('wu')
            w16, u16 = wu[:, :PAIR], wu[:, PAIR:]
            aw = jnp.dot(it.pop('a_rb'), jnp.concatenate([_stack_heads(w16), _stack_heads(u16)], axis=1),
                         preferred_element_type=F32)
            q_scr[i] = (it.pop('r_t') - aw[:, :PAIR]).astype(BF16)
            o0_scr[i] = it.pop('av')[CHUNK:] - aw[:, PAIR:]
            b_e = it.pop('b_e')
            g = lax.dot_general(w16, b_e, (((0,), (0,)), ((), ())), preferred_element_type=F32)
            g_scr[i] = jnp.where(same_head, g, 0.0).astype(BF16)
            h = lax.dot_general(jnp.concatenate([it.pop('v'), u16], axis=0),
                                jnp.concatenate([it.pop('k_e'), -b_e], axis=0),
                                (((0,), (0,)), ((), ())), preferred_element_type=F32)
            h_scr[i] = jnp.where(same_head, h, 0.0)

    for c in range(n_chunks):
        cc = (n_chunks - 1 - c) if rev else c
        rows = slice(cc * CHUNK, (cc + 1) * CHUNK)
        for p in range(n_pairs):
            i = cc * n_pairs + p
            ln = slice(p * PAIR, (p + 1) * PAIR)
            s_bd = s_scr[p]
            s16 = s_bd.astype(BF16)
            o = lax.dot_general(q_scr[i], s16, (((1,), (1,)), ((), ())), preferred_element_type=F32)
            out_buf[rows, ln] = o + o0_scr[i]
            s_scr[p] = (s_bd * dec_scr[cc * 8:cc * 8 + 1, ln]
                        - jnp.dot(s16, g_scr[i], preferred_element_type=F32) + h_scr[i])

    if post:
        r = pa_ref[:, 0:D_A]
        o = of_ref[...] + ob_scr[...]
        mu = _head_sums(o, ind) * (1.0 / HEAD)
        d = o - mu
        var = _head_sums(d * d, ind) * (1.0 / HEAD)
        o = d * lax.rsqrt(var + GN_EPS) * lnw_ref[...] + lnb_ref[...]
        bonus = _head_sums(r * pa_ref[:, D_A:2 * D_A] * rk_ref[...], ind) * pa_ref[:, 2 * D_A:3 * D_A]
        gate = _dot(_sigmoid(pa_ref[:, 3 * D_A + 2 * PAIR:3 * D_A + 2 * PAIR + LORA_G]), gup_ref[...])
        o_ref[...] = ((o + bonus) * gate).astype(o_ref.dtype)

    if want_final:
        @pl.when(j == n_blk - 1)
        def _():
            _store_state_pairs(sfin_ref, s_scr, n_pairs, transpose=False)


def _blk_map(n_blk, rev):
    if rev:
        return lambda b, j: (b * n_blk + n_blk - 1 - j, 0)
    return lambda b, j: (b * n_blk + j, 0)


def _row(v):
    return v.reshape(1, -1)


def _rwkv_dir(pa, o_fwd, s0, wts, *, batch, seq_len, rev, want_final):
    n = pa.shape[0]
    n_blk = seq_len // BLOCK_T
    post = rev
    has_init = s0 is not None
    blk = _blk_map(n_blk, rev)
    const = lambda b, j: (0, 0)
    state_spec = pl.BlockSpec((None, D_A // HEAD, HEAD, HEAD), lambda b, j: (b, 0, 0, 0))
    d = 1 if rev else 0
    ins, specs = [pa], [pl.BlockSpec((BLOCK_T, PA_COLS), blk)]
    if post:
        ins.append(o_fwd)
        specs.append(pl.BlockSpec((BLOCK_T, D_A), blk))
    if has_init:
        ins.append(s0)
        specs.append(state_spec)
    small = [_row(wts['w0'][d]), wts['w_up_pad'][d], _row(wts['a0'][d]), wts['a_up_pad'][d],
             _row(wts['k_k']), _row(wts['k_a'])]
    if post:
        small += [wts['g_up'], _row(wts['r_k']), _row(wts['ln_w']), _row(wts['ln_b'])]
    for wgt in small:
        ins.append(wgt)
        specs.append(pl.BlockSpec(wgt.shape, const))
    out_shape = [jax.ShapeDtypeStruct((n, D_A), BF16 if post else F32)]
    out_specs = [pl.BlockSpec((BLOCK_T, D_A), blk)]
    if want_final:
        out_shape.append(jax.ShapeDtypeStruct((batch, D_A // HEAD, HEAD, HEAD), F32))
        out_specs.append(state_spec)
    n_items = (BLOCK_T // CHUNK) * (D_A // PAIR)
    scratch = [pltpu.VMEM((D_A // PAIR, PAIR, PAIR), F32)] + [pltpu.VMEM((BLOCK_T, D_A), F32)] * 4
    scratch += [pltpu.VMEM((n_items, CHUNK, PAIR), BF16), pltpu.VMEM((n_items, PAIR, PAIR), BF16),
                pltpu.VMEM((n_items, PAIR, PAIR), F32), pltpu.VMEM((n_items, CHUNK, PAIR), F32),
                pltpu.VMEM((8 * BLOCK_T // CHUNK, D_A), F32)]
    if post:
        scratch.append(pltpu.VMEM((BLOCK_T, D_A), F32))
    res = pl.pallas_call(
        functools.partial(_rwkv_kernel, rev=rev, has_init=has_init, want_final=want_final, post=post,
                          n_blk=n_blk),
        grid=(batch, n_blk),
        in_specs=specs, out_specs=out_specs, out_shape=out_shape, scratch_shapes=scratch,
        compiler_params=pltpu.CompilerParams(vmem_limit_bytes=VMEM_LIMIT),
        name="rwkv_bwd" if rev else "rwkv_fwd",
    )(*ins)
    return res[0], (res[1] if want_final else None)


DIAG = 4
HGRN_LEVELS = (4, 8, 16, 32)


def _block_edges(b, rev):
    t = _iota2(b.shape, 0)
    e = b
    out = {}
    for kbit in (1, 2, 4, 8, 16):
        if rev:
            e = jnp.where((t & kbit) != 0, pltpu.roll(e, kbit, 0), e)
        else:
            e = jnp.where((t & kbit) == 0, pltpu.roll(e, CHUNK - kbit, 0), e)
        if 2 * kbit in HGRN_LEVELS:
            out[2 * kbit] = e
    return out


def _hgrn_kernel(*refs, rev, has_init, want_final, post, n_blk):
    refs = list(refs)
    pc_ref = refs.pop(0)
    of_ref = refs.pop(0) if post else None
    s0_ref = refs.pop(0) if has_init else None
    lb_ref = refs.pop(0)
    ng_ref = refs.pop(0) if post else None
    o_ref = refs.pop(0)
    sfin_ref = refs.pop(0) if want_final else None
    s_scr, q_scr, logf_scr, kf_scr, f_scr, qe_scr, ds_scr, o0_scr, dec_scr = (refs.pop(0) for _ in range(9))
    ob_scr = refs.pop(0) if post else None

    n_pairs = D_C // PAIR
    j = pl.program_id(1)
    ind = _head_sum_matrix()

    @pl.when(j == 0)
    def _():
        if has_init:
            _load_state_pairs(s0_ref, s_scr, n_pairs, transpose=True)
        else:
            s_scr[...] = jnp.zeros(s_scr.shape, F32)

    q_scr[...] = _silu(pc_ref[:, 0:D_C])
    f_col = 2 * D_C if rev else D_C
    lb = lb_ref[...]
    f = lb + (1.0 - lb) * _sigmoid(pc_ref[:, f_col:f_col + D_C])
    f_scr[...] = f
    logf_scr[...] = jnp.log(f)
    kf_scr[...] = 1.0 - f

    tri = _cumsum_matrix(rev)
    same_head = _pair_masks(rev)[0]
    t_row, s_lane = _iota2((CHUNK, PAIR), 0), _iota2((CHUNK, PAIR), 1) & (CHUNK - 1)
    level_mask = {}
    for h in HGRN_LEVELS:
        sh = h.bit_length() - 1
        step = -1 if rev else 1
        level_mask[h] = ((t_row >> sh) == (s_lane >> sh) + step) & ((t_row >> (sh + 1)) == (s_lane >> (sh + 1)))
    out_buf = ob_scr if post else o_ref

    n_chunks = BLOCK_T // CHUNK

    for c0 in range(0, n_chunks, PHASE_A_CHUNKS):
        items = []
        for c in range(c0, c0 + PHASE_A_CHUNKS):
            rows = slice(c * CHUNK, (c + 1) * CHUNK)
            b_all = _dot_left01(tri, logf_scr[rows, :])
            for p in range(n_pairs):
                ln = slice(p * PAIR, (p + 1) * PAIR)
                b = b_all[:, ln]
                items.append(dict(
                    idx=c * n_pairs + p, b=b, b_last=b[0:1, :] if rev else b[CHUNK - 1:CHUNK, :],
                    q=q_scr[rows, ln], kf=kf_scr[rows, ln], f=f_scr[rows, ln],
                    v=pc_ref[rows, 3 * D_C + p * PAIR:3 * D_C + (p + 1) * PAIR]))
        for it in items:
            b, q, kf = it['b'], it['q'], it['kf']
            edges = _block_edges(b, rev)
            score = jnp.zeros((CHUNK, PAIR), F32)
            for h in HGRN_LEVELS:
                own = edges[h]
                partner = pltpu.roll(own, (CHUNK - h) if rev else h, 0)
                kp = (kf * jnp.exp(jnp.minimum(own - b, 0.0))).astype(BF16)
                qp = q * jnp.exp(jnp.minimum(b - partner, 0.0))
                sc = _dot_nt(qp, _stack_heads(kp))
                score = jnp.where(level_mask[h], sc, score)
            it['score'] = score.astype(BF16)
        for it in items:
            i = it['idx']
            v16 = it['v'].astype(BF16)
            qe_scr[i] = (it['q'] * jnp.exp(it['b'])).astype(BF16)
            k_e = (it['kf'] * jnp.exp(it['b_last'] - it['b'])).astype(BF16)
            ds = lax.dot_general(v16, k_e, (((0,), (0,)), ((), ())), preferred_element_type=F32)
            ds_scr[i] = jnp.where(same_head, ds, 0.0)
            dec_scr[i * 8:i * 8 + 1, :] = jnp.exp(it['b_last'])
            it['o'] = jnp.dot(it.pop('score'), _stack_heads(v16), preferred_element_type=F32)
        for delta in range(DIAG):
            for it in items:
                q, kf, f, v = it['q'], it['kf'], it['f'], it['v']
                if delta == 0:
                    x, v_s = q * kf, v
                else:
                    roll = lambda z, d: z if d == 0 else pltpu.roll(z, (CHUNK - d) if rev else d, 0)
                    it['e'] = f if delta == 1 else it['e'] * roll(f, delta - 1)
                    x = q * it['e'] * roll(kf, delta)
                    ok = ((t_row & (DIAG - 1)) <= DIAG - 1 - delta) if rev else ((t_row & (DIAG - 1)) >= delta)
                    x = jnp.where(ok, x, 0.0)
                    v_s = roll(v, delta)
                it['o'] = it['o'] + _dot(x, ind) * v_s
        for it in items:
            o0_scr[it['idx']] = it['o']

    for c in range(n_chunks):
        cc = (n_chunks - 1 - c) if rev else c
        rows = slice(cc * CHUNK, (cc + 1) * CHUNK)
        for p in range(n_pairs):
            i = cc * n_pairs + p
            s_bd = s_scr[p]
            o = lax.dot_general(qe_scr[i], s_bd.astype(BF16), (((1,), (1,)), ((), ())),
                                preferred_element_type=F32)
            out_buf[rows, p * PAIR:(p + 1) * PAIR] = o + o0_scr[i]
            s_scr[p] = s_bd * dec_scr[i * 8:i * 8 + 1, :] + ds_scr[i]

    if post:
        o = of_ref[...] + ob_scr[...]
        ms = _head_sums(o * o, _head_sum_matrix(D_C)) * (1.0 / HEAD)
        o = o * lax.rsqrt(ms + RMS_EPS) * ng_ref[...] * _silu(pc_ref[:, 4 * D_C:5 * D_C])
        o_ref[...] = o.astype(o_ref.dtype)

    if want_final:
        @pl.when(j == n_blk - 1)
        def _():
            _store_state_pairs(sfin_ref, s_scr, n_pairs, transpose=True)


def _hgrn_dir(pc, o_fwd, s0, lb, norm_g, *, batch, seq_len, rev, want_final):
    n = pc.shape[0]
    n_blk = seq_len // BLOCK_T
    post = rev
    has_init = s0 is not None
    blk = _blk_map(n_blk, rev)
    const = lambda b, j: (0, 0)
    state_spec = pl.BlockSpec((None, D_C // HEAD, HEAD, HEAD), lambda b, j: (b, 0, 0, 0))
    ins, specs = [pc], [pl.BlockSpec((BLOCK_T, PC_COLS), blk)]
    if post:
        ins.append(o_fwd)
        specs.append(pl.BlockSpec((BLOCK_T, D_C), blk))
    if has_init:
        ins.append(s0)
        specs.append(state_spec)
    ins.append(_row(lb))
    specs.append(pl.BlockSpec((1, D_C), const))
    if post:
        ins.append(_row(norm_g))
        specs.append(pl.BlockSpec((1, D_C), const))
    out_shape = [jax.ShapeDtypeStruct((n, D_C), BF16 if post else F32)]
    out_specs = [pl.BlockSpec((BLOCK_T, D_C), blk)]
    if want_final:
        out_shape.append(jax.ShapeDtypeStruct((batch, D_C // HEAD, HEAD, HEAD), F32))
        out_specs.append(state_spec)
    n_items = (BLOCK_T // CHUNK) * (D_C // PAIR)
    scratch = [pltpu.VMEM((D_C // PAIR, PAIR, PAIR), F32)] + [pltpu.VMEM((BLOCK_T, D_C), F32)] * 4
    scratch += [pltpu.VMEM((n_items, CHUNK, PAIR), BF16), pltpu.VMEM((n_items, PAIR, PAIR), F32),
                pltpu.VMEM((n_items, CHUNK, PAIR), F32), pltpu.VMEM((8 * n_items, PAIR), F32)]
    if post:
        scratch.append(pltpu.VMEM((BLOCK_T, D_C), F32))
    res = pl.pallas_call(
        functools.partial(_hgrn_kernel, rev=rev, has_init=has_init, want_final=want_final, post=post,
                          n_blk=n_blk),
        grid=(batch, n_blk),
        in_specs=specs, out_specs=out_specs, out_shape=out_shape, scratch_shapes=scratch,
        compiler_params=pltpu.CompilerParams(vmem_limit_bytes=VMEM_LIMIT),
        name="hgrn_bwd" if rev else "hgrn_fwd",
    )(*ins)
    return res[0], (res[1] if want_final else None)


def _shift_rows(x, s, t_idx):
    n = x.shape[0]
    y = pltpu.roll(x, s % n, 0)
    return jnp.where((t_idx >= s) if s > 0 else (t_idx < n + s), y, 0.0)


SCAN_GROUP = 8


def _group_scan(a, b, t_idx, rev):
    n = a.shape[0]
    in_group = t_idx & (SCAN_GROUP - 1)
    k = 1
    while k < SCAN_GROUP:
        ok = (in_group < SCAN_GROUP - k) if rev else (in_group >= k)
        sh = (n - k) if rev else k
        b = jnp.where(ok, a * pltpu.roll(b, sh, 0) + b, b)
        a = jnp.where(ok, a * pltpu.roll(a, sh, 0), a)
        k *= 2
    return a, b


def _gelu_tanh(x):
    return 0.5 * x * (1.0 + jnp.tanh(math.sqrt(2.0 / math.pi) * (x + 0.044715 * x * x * x)))


def _lru_kernel(*refs, has_init, want_final):
    refs = list(refs)
    pb_ref = refs.pop(0)
    h0_ref = refs.pop(0) if has_init else None
    cw_ref, cb_ref, wa_ref, ba_ref, wx_ref, bx_ref, lam_ref = (refs.pop(0) for _ in range(7))
    y_ref = refs.pop(0)
    hfin_ref = refs.pop(0) if want_final else None
    a_scr, b_scr, h_scr = (refs.pop(0) for _ in range(3))

    n = pb_ref.shape[0]
    t_idx = _iota2((n, D_B), 0)
    xb = pb_ref[:, 0:D_B]
    u = cb_ref[...] + cw_ref[2:3, :] * xb
    u = u + cw_ref[0:1, :] * _shift_rows(xb, 2, t_idx)
    u = u + cw_ref[1:2, :] * _shift_rows(xb, 1, t_idx)
    u = u + cw_ref[3:4, :] * _shift_rows(xb, -1, t_idx)
    for d in range(2):
        rg = _sigmoid(_dot(u, wa_ref[d]) + ba_ref[d:d + 1, :])
        ig = _sigmoid(_dot(u, wx_ref[d]) + bx_ref[d:d + 1, :])
        log_a = -RG_C * rg * _softplus(-lam_ref[d:d + 1, :])
        a_t = jnp.exp(log_a)
        in_scale = jnp.sqrt((1.0 + a_t * a_t) * jnp.tanh(-log_a))
        a_scr[d], b_scr[d] = _group_scan(a_t, in_scale * (ig * u), t_idx, rev=d == 1)

    n_groups = n // SCAN_GROUP

    def carry_body(g, carry):
        h_f, h_b = carry
        rows_f = pl.ds(pl.multiple_of(g * SCAN_GROUP, SCAN_GROUP), SCAN_GROUP)
        rows_b = pl.ds(pl.multiple_of((n_groups - 1 - g) * SCAN_GROUP, SCAN_GROUP), SCAN_GROUP)
        blk_f = b_scr[0, rows_f, :] + a_scr[0, rows_f, :] * h_f
        blk_b = b_scr[1, rows_b, :] + a_scr[1, rows_b, :] * h_b
        h_scr[0, rows_f, :] = blk_f
        h_scr[1, rows_b, :] = blk_b
        return blk_f[SCAN_GROUP - 1:SCAN_GROUP, :], blk_b[0:1, :]

    if has_init:
        init = (h0_ref[0:1, :], h0_ref[1:2, :])
    else:
        init = (jnp.zeros((1, D_B), F32), jnp.zeros((1, D_B), F32))
    h_f, h_b = lax.fori_loop(0, n_groups, carry_body, init, unroll=4)
    if want_final:
        hfin_ref[0:1, :] = h_f
        hfin_ref[1:2, :] = h_b
    y_ref[...] = ((h_scr[0] + h_scr[1]) * _gelu_tanh(pb_ref[:, D_B:2 * D_B])).astype(y_ref.dtype)


def _lru(pb, h0, wts, *, batch, seq_len, want_final):
    n = pb.shape[0]
    has_init = h0 is not None
    const2 = lambda b: (0, 0)
    const3 = lambda b: (0, 0, 0)
    ins, specs = [pb], [pl.BlockSpec((seq_len, PB_COLS), lambda b: (b, 0))]
    state_spec = pl.BlockSpec((None, 2, D_B), lambda b: (b, 0, 0))
    if has_init:
        ins.append(h0)
        specs.append(state_spec)
    for wgt in (wts['conv_w'], _row(wts['conv_b']), wts['wa_bd'], wts['ba'], wts['wx_bd'], wts['bx'],
                wts['lam']):
        ins.append(wgt)
        specs.append(pl.BlockSpec(wgt.shape, const3 if wgt.ndim == 3 else const2))
    out_shape = [jax.ShapeDtypeStruct((n, D_B), BF16)]
    out_specs = [pl.BlockSpec((seq_len, D_B), lambda b: (b, 0))]
    if want_final:
        out_shape.append(jax.ShapeDtypeStruct((batch, 2, D_B), F32))
        out_specs.append(state_spec)
    res = pl.pallas_call(
        functools.partial(_lru_kernel, has_init=has_init, want_final=want_final),
        grid=(batch,),
        in_specs=specs, out_specs=out_specs, out_shape=out_shape,
        scratch_shapes=[pltpu.VMEM((2, seq_len, D_B), F32)] * 3,
        compiler_params=pltpu.CompilerParams(vmem_limit_bytes=VMEM_LIMIT),
        name="lru",
    )(*ins)
    return res[0], (res[1] if want_final else None)


def _out_ffn_kernel(*refs, conv_period, final):
    refs = list(refs)
    x_ref, ma_ref, mb_ref, mc_ref, mod_ref, g_ref, wo_ref, wg_ref, wu_ref, wd_ref, cw_ref, cb_ref = (
        refs.pop(0) for _ in range(12))
    fg_ref = refs.pop(0) if final else None
    xo_ref = refs.pop(0)
    yo_ref = refs.pop(0) if final else None

    dm = D_MODEL
    mix = jnp.dot(ma_ref[...], wo_ref[0:D_A, :], preferred_element_type=F32)
    mix = mix + jnp.dot(mb_ref[...], wo_ref[D_A:D_A + D_B, :], preferred_element_type=F32)
    mix = mix + jnp.dot(mc_ref[...], wo_ref[D_A + D_B:, :], preferred_element_type=F32)
    x = x_ref[...] + mod_ref[:, 2 * dm:3 * dm] * mix
    h = _rms(x) * g_ref[...]
    h = (h * (1.0 + mod_ref[:, 4 * dm:5 * dm]) + mod_ref[:, 3 * dm:4 * dm]).astype(BF16)
    tm = x.shape[0]
    t_idx = _iota2((tm, FF_TILE), 0) & (conv_period - 1)
    acc = jnp.zeros((tm, dm), F32)
    for f0 in range(0, D_FF, FF_TILE):
        cols = slice(f0, f0 + FF_TILE)
        g = jnp.dot(h, wg_ref[:, cols], preferred_element_type=F32)
        g_prev = jnp.where(t_idx == 0, 0.0, pltpu.roll(g, 1, 0))
        g_next = jnp.where(t_idx == conv_period - 1, 0.0, pltpu.roll(g, tm - 1, 0))
        g = cw_ref[0:1, cols] * g_prev + cw_ref[1:2, cols] * g + cw_ref[2:3, cols] * g_next + cb_ref[:, cols]
        up = jnp.dot(h, wu_ref[:, cols], preferred_element_type=F32)
        acc = acc + jnp.dot((_silu(g) * up).astype(BF16), wd_ref[cols, :], preferred_element_type=F32)
    x = x + mod_ref[:, 5 * dm:6 * dm] * acc
    xo_ref[...] = x
    if final:
        yo_ref[...] = _rms(x) * fg_ref[...]


def _out_ffn(x, mix_a, mix_b, mix_c, mods, wts, *, seq_len, conv_period, final_g):
    n = x.shape[0]
    final = final_g is not None
    tiles_per_seq = seq_len // TOKEN_TILE if mods.shape[0] > 1 else None
    mod_map = (lambda i: (i // tiles_per_seq, 0, 0)) if tiles_per_seq else (lambda i: (0, 0, 0))
    const = lambda i: (0, 0)
    tok = lambda width: pl.BlockSpec((TOKEN_TILE, width), lambda i: (i, 0))
    resident = lambda w: _layer_resident(w, wts['layer'])
    ins = [x, mix_a, mix_b, mix_c, mods, _row(wts['norm_g']), wts['w_out'], wts['w_gate'], wts['w_up'],
           wts['w_down'], wts['conv_w'], _row(wts['conv_b'])]
    specs = [tok(D_MODEL), tok(D_A), tok(D_B), tok(D_C), pl.BlockSpec((None, 1, N_MOD * D_MODEL), mod_map),
             pl.BlockSpec((1, D_MODEL), const), resident(wts['w_out']), resident(wts['w_gate']),
             resident(wts['w_up']), resident(wts['w_down']), pl.BlockSpec(wts['conv_w'].shape, const),
             pl.BlockSpec((1, D_FF), const)]
    out_shape = [jax.ShapeDtypeStruct((n, D_MODEL), F32)]
    out_specs = [tok(D_MODEL)]
    if final:
        ins.append(_row(final_g))
        specs.append(pl.BlockSpec((1, D_MODEL), const))
        out_shape.append(jax.ShapeDtypeStruct((n, D_MODEL), F32))
        out_specs.append(tok(D_MODEL))
    res = pl.pallas_call(
        functools.partial(_out_ffn_kernel, conv_period=conv_period, final=final),
        grid=(n // TOKEN_TILE,),
        in_specs=specs, out_specs=out_specs, out_shape=out_shape,
        compiler_params=pltpu.CompilerParams(vmem_limit_bytes=VMEM_LIMIT),
        name="out_ffn",
    )(*ins)
    return res[0], (res[1] if final else None)


def _trunk_layer(x, mods, s_rwkv, s_lru, s_hgrn, wts, *, batch, seq_len, conv_period, want_final, final_g):
    pa, pb, pc = _norm_in(x, mods, _row(wts['norm_mix_g']), wts['w_in'], wts['layer'], seq_len)
    kw = dict(batch=batch, seq_len=seq_len, want_final=want_final)
    sel = lambda s, d: None if s is None else s[:, d]
    o_f, sa_f = _rwkv_dir(pa, None, sel(s_rwkv, 0), wts['rwkv'], rev=False, **kw)
    mix_a, sa_b = _rwkv_dir(pa, o_f, sel(s_rwkv, 1), wts['rwkv'], rev=True, **kw)
    c_f, sc_f = _hgrn_dir(pc, None, sel(s_hgrn, 0), wts['hgrn_lb'][0], wts['hgrn_norm_g'], rev=False, **kw)
    mix_c, sc_b = _hgrn_dir(pc, c_f, sel(s_hgrn, 1), wts['hgrn_lb'][1], wts['hgrn_norm_g'], rev=True, **kw)
    mix_b, sb = _lru(pb, s_lru, wts['lru'], **kw)
    x, y = _out_ffn(x, mix_a, mix_b, mix_c, mods, wts['ffn'], seq_len=seq_len, conv_period=conv_period,
                    final_g=final_g)
    states = None
    if want_final:
        states = ([sa_f, sa_b], sb, [sc_f, sc_b])
    return x, y, states


def _block_diag(w):
    nd, nb, bi, bj = w.shape
    eye = jnp.eye(nb, dtype=w.dtype)
    return jnp.einsum('dhij,hg->dhigj', w, eye).reshape(nd, nb * bi, nb * bj)


def _pad_dir_rows(w):
    z = jnp.zeros_like(w[0])
    return jnp.stack([jnp.concatenate([w[0], z], axis=0), jnp.concatenate([z, w[1]], axis=0)], axis=0)


def kernel(x_prompt, x_sample, state_rwkv, state_rglru, state_hgrn, c, c_ctx, ada_w, ada_b, norm_mix_g, norm_ffn_g, w_in, w_out, rwkv_w0, rwkv_w_up, rwkv_a0, rwkv_a_up, rwkv_g_up, rwkv_k_k, rwkv_k_a, rwkv_r_k, rwkv_ln_w, rwkv_ln_b, lru_conv_w, lru_conv_b, lru_wa, lru_ba, lru_wx, lru_bx, lru_lambda, hgrn_lb_logits, hgrn_norm_g, ffn_w_gate, ffn_w_up, ffn_conv_w, ffn_conv_b, ffn_w_down, final_g):
    n_ctx, t_ctx, dm = x_prompt.shape
    n_lat, t_lat, _ = x_sample.shape
    depth = w_in.shape[0]
    grid_w = 64

    cond = jnp.concatenate([c_ctx[None, :], c, jnp.zeros((8 - 1 - n_lat, dm), F32)], axis=0)
    mods = _modulation(cond, ada_w, ada_b.reshape(depth, 1, -1))
    lb_all = jnp.cumsum(jax.nn.softmax(hgrn_lb_logits.astype(F32), axis=1), axis=1)
    lb_all = lb_all - lb_all[:, :1]

    xp = x_prompt.reshape(n_ctx * t_ctx, dm)
    xs = x_sample.reshape(n_lat * t_lat, dm)
    new_rwkv, new_lru, new_hgrn = [], [], []
    yp = ys = None
    w_in16, w_out16 = _to_bf16(w_in), _to_bf16(w_out)
    w_gate16, w_up16, w_down16 = _to_bf16(ffn_w_gate), _to_bf16(ffn_w_up), _to_bf16(ffn_w_down)
    for l in range(depth):
        wts = {
            'norm_mix_g': norm_mix_g[l], 'layer': l,
            'w_in': w_in16,
            'rwkv': {'w0': rwkv_w0[l], 'w_up_pad': _pad_dir_rows(rwkv_w_up[l]).astype(BF16),
                     'a0': rwkv_a0[l], 'a_up_pad': _pad_dir_rows(rwkv_a_up[l]).astype(BF16),
                     'g_up': rwkv_g_up[l].astype(BF16), 'k_k': rwkv_k_k[l], 'k_a': rwkv_k_a[l],
                     'r_k': rwkv_r_k[l], 'ln_w': rwkv_ln_w[l], 'ln_b': rwkv_ln_b[l]},
            'lru': {'conv_w': lru_conv_w[l], 'conv_b': lru_conv_b[l],
                    'wa_bd': _block_diag(lru_wa[l]).astype(BF16), 'ba': lru_ba[l],
                    'wx_bd': _block_diag(lru_wx[l]).astype(BF16), 'bx': lru_bx[l], 'lam': lru_lambda[l]},
            'hgrn_lb': lb_all[:, l], 'hgrn_norm_g': hgrn_norm_g[l],
            'ffn': {'norm_g': norm_ffn_g[l], 'layer': l, 'w_out': w_out16, 'w_gate': w_gate16, 'w_up': w_up16,
                    'w_down': w_down16, 'conv_w': ffn_conv_w[l], 'conv_b': ffn_conv_b[l]},
        }
        last = l == depth - 1
        fg = final_g if last else None
        xp, yp, st = _trunk_layer(xp, mods[l, 0:1].reshape(1, 1, -1), None, None, None, wts,
                                  batch=n_ctx, seq_len=t_ctx, conv_period=t_ctx, want_final=True, final_g=fg)
        new_rwkv += st[0]
        new_lru.append(st[1])
        new_hgrn += st[2]
        xs, ys, _ = _trunk_layer(xs, mods[l, 1:1 + n_lat].reshape(n_lat, 1, -1),
                                 state_rwkv[:, l].astype(F32), state_rglru[:, l].astype(F32),
                                 state_hgrn[:, l].astype(F32), wts,
                                 batch=n_lat, seq_len=t_lat, conv_period=grid_w, want_final=False, final_g=fg)
    return (yp.reshape(n_ctx, t_ctx, dm), ys.reshape(n_lat, t_lat, dm),
            jnp.stack(new_rwkv, axis=1).reshape((n_ctx, depth, 2) + new_rwkv[0].shape[1:]),
            jnp.stack(new_lru, axis=1),
            jnp.stack(new_hgrn, axis=1).reshape((n_ctx, depth, 2) + new_hgrn[0].shape[1:]))
```

```python
import functools
import math

import jax
import jax.numpy as jnp
from jax import lax
from jax.experimental import pallas as pl
from jax.experimental.pallas import tpu as pltpu

F32 = jnp.float32
BF16 = jnp.bfloat16

D_MODEL = 1024
HEAD = 64
PAIR = 2 * HEAD
D_A = 512
D_B = 256
D_C = 256
LORA_G = 128
N_MOD = 6
D_FF = 2816
FF_TILE = 256
PA_COLS = 3 * D_A + 4 * 64 + LORA_G
PB_COLS = 2 * D_B
PC_COLS = 5 * D_C
CHUNK = 64
BLOCK_T = 256
PHASE_A_CHUNKS = 4
TOKEN_TILE = 512
CAST_ROWS = 256
RMS_EPS = 1e-6
GN_EPS = 64e-5
RG_C = 8.0
VMEM_LIMIT = 48 * 1024 * 1024


def _dot(a, b):
    return jnp.dot(a.astype(BF16), b.astype(BF16), preferred_element_type=F32)


def _dot_nt(a, b):
    return lax.dot_general(a.astype(BF16), b.astype(BF16), (((1,), (1,)), ((), ())),
                           preferred_element_type=F32)


def _split3(x):
    hi = x.astype(BF16)
    r1 = x - hi.astype(F32)
    mid = r1.astype(BF16)
    return hi, mid, (r1 - mid.astype(F32)).astype(BF16)


def _dot_left01(m01, x):
    return sum(jnp.dot(m01, t, preferred_element_type=F32) for t in _split3(x))


def _dot_right01(x, m01):
    hi = x.astype(BF16)
    lo = (x - hi.astype(F32)).astype(BF16)
    return jnp.dot(hi, m01, preferred_element_type=F32) + jnp.dot(lo, m01, preferred_element_type=F32)


def _sigmoid(x):
    return 1.0 / (1.0 + jnp.exp(-x))


def _silu(x):
    return x * _sigmoid(x)


def _softplus(x):
    return jnp.maximum(x, 0.0) + jnp.log1p(jnp.exp(-jnp.abs(x)))


def _iota2(shape, axis):
    return lax.broadcasted_iota(jnp.int32, shape, axis)


def _head_sum_matrix(n=PAIR):
    return ((_iota2((n, n), 0) >> 6) == (_iota2((n, n), 1) >> 6)).astype(BF16)


def _head_sums(x, ind):
    n = ind.shape[0]
    return jnp.concatenate([_dot_right01(x[:, i:i + n], ind) for i in range(0, x.shape[1], n)], axis=1)


def _cumsum_matrix(rev):
    t, s = _iota2((CHUNK, CHUNK), 0), _iota2((CHUNK, CHUNK), 1)
    return ((s >= t) if rev else (s <= t)).astype(BF16)


def _stack_heads(x):
    first = _iota2(x.shape, 1) < HEAD
    return jnp.concatenate([jnp.where(first, x, 0.0), jnp.where(first, 0.0, x)], axis=0)


def _pair_masks(rev):
    i, j = _iota2((PAIR, PAIR), 0), _iota2((PAIR, PAIR), 1)
    same = (i >> 6) == (j >> 6)
    t, s = i & (CHUNK - 1), j & (CHUNK - 1)
    strict = same & ((s > t) if rev else (s < t))
    incl = same & ((s >= t) if rev else (s <= t))
    return same, strict, incl, t, s


def _load_state_pairs(s0_ref, s_scr, n_pairs, transpose):
    z = jnp.zeros((HEAD, HEAD), F32)
    for p in range(n_pairs):
        a, b = s0_ref[2 * p], s0_ref[2 * p + 1]
        if transpose:
            a, b = a.T, b.T
        s_scr[p] = jnp.concatenate([jnp.concatenate([a, z], axis=1), jnp.concatenate([z, b], axis=1)], axis=0)


def _store_state_pairs(sfin_ref, s_scr, n_pairs, transpose):
    for p in range(n_pairs):
        s = s_scr[p]
        a, b = s[:HEAD, :HEAD], s[HEAD:, HEAD:]
        if transpose:
            a, b = a.T, b.T
        sfin_ref[2 * p] = a
        sfin_ref[2 * p + 1] = b


def _cast_kernel(w_ref, o_ref):
    o_ref[...] = w_ref[...].astype(o_ref.dtype)


def _to_bf16(w):
    n_layer, k, n = w.shape
    rows = CAST_ROWS if k % CAST_ROWS == 0 else k
    spec = pl.BlockSpec((None, rows, n), lambda l, i: (l, i, 0))
    return pl.pallas_call(
        _cast_kernel, grid=(n_layer, k // rows), in_specs=[spec], out_specs=spec,
        out_shape=jax.ShapeDtypeStruct(w.shape, BF16), name="cast_bf16",
    )(w)


def _mod_kernel(c_ref, w_ref, b_ref, o_ref):
    c, w = _silu(c_ref[...]), w_ref[...]
    c_hi, w_hi = c.astype(BF16), w.astype(BF16)
    c_lo, w_lo = (c - c_hi.astype(F32)).astype(BF16), (w - w_hi.astype(F32)).astype(BF16)
    d = lambda a, b: jnp.dot(a, b, preferred_element_type=F32)
    o_ref[...] = d(c_hi, w_hi) + (d(c_hi, w_lo) + d(c_lo, w_hi)) + b_ref[...]


def _modulation(cond, ada_w, ada_b):
    n_layer, _, n_out = ada_w.shape
    tn = 1024
    return pl.pallas_call(
        _mod_kernel,
        grid=(n_layer, n_out // tn),
        in_specs=[pl.BlockSpec((8, D_MODEL), lambda l, j: (0, 0)),
                  pl.BlockSpec((None, D_MODEL, tn), lambda l, j: (l, 0, j)),
                  pl.BlockSpec((None, 1, tn), lambda l, j: (l, 0, j))],
        out_specs=pl.BlockSpec((None, 8, tn), lambda l, j: (l, 0, j)),
        out_shape=jax.ShapeDtypeStruct((n_layer, 8, n_out), F32),
        name="modulation",
    )(cond, ada_w, ada_b)


def _rms(x):
    return x * lax.rsqrt(jnp.mean(x * x, axis=-1, keepdims=True) + RMS_EPS)


def _norm_in_kernel(x_ref, mod_ref, g_ref, w_ref, pa_ref, pb_ref, pc_ref):
    h = _rms(x_ref[...]) * g_ref[...]
    h = (h * (1.0 + mod_ref[:, D_MODEL:2 * D_MODEL]) + mod_ref[:, 0:D_MODEL]).astype(BF16)
    pa_ref[...] = jnp.dot(h, w_ref[:, 0:PA_COLS], preferred_element_type=F32)
    pb_ref[...] = jnp.dot(h, w_ref[:, PA_COLS:PA_COLS + PB_COLS], preferred_element_type=F32)
    pc_ref[...] = jnp.dot(h, w_ref[:, PA_COLS + PB_COLS:], preferred_element_type=F32)


def _layer_resident(w, layer):
    return pl.BlockSpec((None,) + w.shape[1:], lambda i: (layer, 0, 0), pipeline_mode=pl.Buffered(1))


def _norm_in(x, mods, g, w_in, layer, seq_len):
    n = x.shape[0]
    tiles_per_seq = seq_len // TOKEN_TILE if mods.shape[0] > 1 else None
    mod_map = (lambda i: (i // tiles_per_seq, 0, 0)) if tiles_per_seq else (lambda i: (0, 0, 0))
    const = lambda i: (0, 0)
    return pl.pallas_call(
        _norm_in_kernel,
        grid=(n // TOKEN_TILE,),
        in_specs=[pl.BlockSpec((TOKEN_TILE, D_MODEL), lambda i: (i, 0)),
                  pl.BlockSpec((None, 1, N_MOD * D_MODEL), mod_map),
                  pl.BlockSpec((1, D_MODEL), const),
                  _layer_resident(w_in, layer)],
        out_specs=[pl.BlockSpec((TOKEN_TILE, PA_COLS), lambda i: (i, 0)),
                   pl.BlockSpec((TOKEN_TILE, PB_COLS), lambda i: (i, 0)),
                   pl.BlockSpec((TOKEN_TILE, PC_COLS), lambda i: (i, 0))],
        out_shape=[jax.ShapeDtypeStruct((n, PA_COLS), F32),
                   jax.ShapeDtypeStruct((n, PB_COLS), F32),
                   jax.ShapeDtypeStruct((n, PC_COLS), F32)],
        compiler_params=pltpu.CompilerParams(vmem_limit_bytes=VMEM_LIMIT),
        name="norm_in",
    )(x, mods, g, w_in)


def _rwkv_kernel(*refs, rev, has_init, want_final, post, n_blk):
    refs = list(refs)
    pa_ref = refs.pop(0)
    of_ref = refs.pop(0) if post else None
    s0_ref = refs.pop(0) if has_init else None
    w0_ref, wup_ref, a0_ref, aup_ref, kkw_ref, kaw_ref = (refs.pop(0) for _ in range(6))
    if post:
        gup_ref, rk_ref, lnw_ref, lnb_ref = (refs.pop(0) for _ in range(4))
    o_ref = refs.pop(0)
    sfin_ref = refs.pop(0) if want_final else None
    s_scr, kd_scr, kk_scr, beta_scr, logw_scr, q_scr, g_scr, h_scr, o0_scr, dec_scr = (
        refs.pop(0) for _ in range(10))
    ob_scr = refs.pop(0) if post else None

    n_pairs = D_A // PAIR
    j = pl.program_id(1)
    ind = _head_sum_matrix(2 * PAIR)

    @pl.when(j == 0)
    def _():
        if has_init:
            _load_state_pairs(s0_ref, s_scr, n_pairs, transpose=False)
        else:
            s_scr[...] = jnp.zeros(s_scr.shape, F32)

    k = pa_ref[:, D_A:2 * D_A]
    wl = w0_ref[...] + _dot(jnp.tanh(pa_ref[:, 3 * D_A:3 * D_A + PAIR]), wup_ref[...])
    logw_scr[...] = -_sigmoid(wl) * math.exp(-0.5)
    a = _sigmoid(a0_ref[...] + _dot(pa_ref[:, 3 * D_A + PAIR:3 * D_A + 2 * PAIR], aup_ref[...]))
    kd_scr[...] = k * (1.0 + (a - 1.0) * kaw_ref[...])
    kk = k * kkw_ref[...]
    kk = kk * lax.rsqrt(_head_sums(kk * kk, ind) + 1e-12)
    kk_scr[...] = kk
    beta_scr[...] = kk * a

    tri = _cumsum_matrix(rev)
    same_head = _pair_masks(rev)[0]
    row_t, lane_s = _iota2((CHUNK, PAIR), 0), _iota2((CHUNK, PAIR), 1) & (CHUNK - 1)
    strict_w = (lane_s > row_t) if rev else (lane_s < row_t)
    incl_w = (lane_s >= row_t) if rev else (lane_s <= row_t)
    ident_w = (lane_s == row_t).astype(F32)
    out_buf = ob_scr if post else o_ref
    n_chunks = BLOCK_T // CHUNK

    for c0 in range(0, n_chunks, PHASE_A_CHUNKS):
        items = []
        for c in range(c0, c0 + PHASE_A_CHUNKS):
            rows = slice(c * CHUNK, (c + 1) * CHUNK)
            logw = logw_scr[rows, :]
            b = _dot_left01(tri, logw)
            b_last = b[0:1, :] if rev else b[CHUNK - 1:CHUNK, :]
            p_inv = jnp.exp(-b)
            p_end = jnp.exp(b_last - b)
            r_t = pa_ref[rows, 0:D_A] * jnp.exp(b)
            k_t = kk_scr[rows, :] * jnp.exp(b - logw)
            kd = kd_scr[rows, :]
            beta = beta_scr[rows, :]
            k_h, b_h = kd * p_inv, beta * p_inv
            k_e, b_e = kd * p_end, beta * p_end
            v = pa_ref[rows, 2 * D_A:3 * D_A]
            dec_scr[c * 8:c * 8 + 1, :] = jnp.exp(b_last)
            for p in range(n_pairs):
                ln = slice(p * PAIR, (p + 1) * PAIR)
                items.append(dict(
                    idx=c * n_pairs + p, k_t=k_t[:, ln].astype(BF16), r_t=r_t[:, ln],
                    k_h=k_h[:, ln].astype(BF16), b_h=b_h[:, ln].astype(BF16), v=v[:, ln].astype(BF16),
                    k_e=k_e[:, ln].astype(BF16), b_e=b_e[:, ln].astype(BF16)))
        for it in items:
            lhs = jnp.concatenate([it['k_t'], it['r_t'].astype(BF16)], axis=0)
            rhs = jnp.concatenate([_stack_heads(it.pop('k_h')), _stack_heads(it.pop('b_h'))], axis=0)
            sc = lax.dot_general(lhs, rhs, (((1,), (1,)), ((), ())), preferred_element_type=F32)
            it['a_kr'] = jnp.concatenate([jnp.where(strict_w, sc[:CHUNK, :PAIR], 0.0),
                                          jnp.where(incl_w, sc[CHUNK:, :PAIR], 0.0)], axis=0).astype(BF16)
            it['a_rb'] = jnp.where(incl_w, sc[CHUNK:, PAIR:], 0.0).astype(BF16)
            it['pw'] = jnp.where(strict_w, -sc[:CHUNK, PAIR:], 0.0)
            it['t'] = ident_w + it['pw']
        for it in items:
            pw16 = it['pw'].astype(BF16)
            it['pw'] = jnp.dot(pw16, _stack_heads(pw16), preferred_element_type=F32).astype(BF16)
        for _ in range(4):
            for it in items:
                m = jnp.dot(jnp.concatenate([it['pw'], it['t'].astype(BF16)], axis=0), _stack_heads(it['pw']),
                            preferred_element_type=F32)
                it['pw'] = m[:CHUNK].astype(BF16)
                it['t'] = it['t'] + m[CHUNK:]
        for it in items:
            it['t'] = (it['t'] + jnp.dot(it['t'].astype(BF16), _stack_heads(it.pop('pw')),
                                         preferred_element_type=F32)).astype(BF16)
            it['av'] = jnp.dot(it.pop('a_kr'), _stack_heads(it['v']), preferred_element_type=F32)
        for it in items:
            y0 = it['av'][:CHUNK].astype(BF16)
            rhs = jnp.concatenate([_stack_heads(it.pop('k_t')), _stack_heads(y0)], axis=1)
            it['wu'] = jnp.dot(it.pop('t'), rhs, preferred_element_type=F32).astype(BF16)
        for it in items:
            i = it['idx']
            wu = it.pop('wu')
            w16, u16 = wu[:, :PAIR], wu[:, PAIR:]
            aw = jnp.dot(it.pop('a_rb'), jnp.concatenate([_stack_heads(w16), _stack_heads(u16)], axis=1),
                         preferred_element_type=F32)
            q_scr[i] = (it.pop('r_t') - aw[:, :PAIR]).astype(BF16)
            o0_scr[i] = it.pop('av')[CHUNK:] - aw[:, PAIR:]
            b_e = it.pop('b_e')
            g = lax.dot_general(w16, b_e, (((0,), (0,)), ((), ())), preferred_element_type=F32)
            g_scr[i] = jnp.where(same_head, g, 0.0).astype(BF16)
            h = lax.dot_general(jnp.concatenate([it.pop('v'), u16], axis=0),
                                jnp.concatenate([it.pop('k_e'), -b_e], axis=0),
                                (((0,), (0,)), ((), ())), preferred_element_type=F32)
            h_scr[i] = jnp.where(same_head, h, 0.0)

    for c in range(n_chunks):
        cc = (n_chunks - 1 - c) if rev else c
        rows = slice(cc * CHUNK, (cc + 1) * CHUNK)
        for p in range(n_pairs):
            i = cc * n_pairs + p
            ln = slice(p * PAIR, (p + 1) * PAIR)
            s_bd = s_scr[p]
            s16 = s_bd.astype(BF16)
            o = lax.dot_general(q_scr[i], s16, (((1,), (1,)), ((), ())), preferred_element_type=F32)
            out_buf[rows, ln] = o + o0_scr[i]
            s_scr[p] = (s_bd * dec_scr[cc * 8:cc * 8 + 1, ln]
                        - jnp.dot(s16, g_scr[i], preferred_element_type=F32) + h_scr[i])

    if post:
        r = pa_ref[:, 0:D_A]
        o = of_ref[...] + ob_scr[...]
        mu = _head_sums(o, ind) * (1.0 / HEAD)
        d = o - mu
        var = _head_sums(d * d, ind) * (1.0 / HEAD)
        o = d * lax.rsqrt(var + GN_EPS) * lnw_ref[...] + lnb_ref[...]
        bonus = _head_sums(r * pa_ref[:, D_A:2 * D_A] * rk_ref[...], ind) * pa_ref[:, 2 * D_A:3 * D_A]
        gate = _dot(_sigmoid(pa_ref[:, 3 * D_A + 2 * PAIR:3 * D_A + 2 * PAIR + LORA_G]), gup_ref[...])
        o_ref[...] = ((o + bonus) * gate).astype(o_ref.dtype)

    if want_final:
        @pl.when(j == n_blk - 1)
        def _():
            _store_state_pairs(sfin_ref, s_scr, n_pairs, transpose=False)


def _blk_map(n_blk, rev):
    if rev:
        return lambda b, j: (b * n_blk + n_blk - 1 - j, 0)
    return lambda b, j: (b * n_blk + j, 0)


def _row(v):
    return v.reshape(1, -1)


def _rwkv_dir(pa, o_fwd, s0, wts, *, batch, seq_len, rev, want_final):
    n = pa.shape[0]
    n_blk = seq_len // BLOCK_T
    post = rev
    has_init = s0 is not None
    blk = _blk_map(n_blk, rev)
    const = lambda b, j: (0, 0)
    state_spec = pl.BlockSpec((None, D_A // HEAD, HEAD, HEAD), lambda b, j: (b, 0, 0, 0))
    d = 1 if rev else 0
    ins, specs = [pa], [pl.BlockSpec((BLOCK_T, PA_COLS), blk)]
    if post:
        ins.append(o_fwd)
        specs.append(pl.BlockSpec((BLOCK_T, D_A), blk))
    if has_init:
        ins.append(s0)
        specs.append(state_spec)
    small = [_row(wts['w0'][d]), wts['w_up_pad'][d], _row(wts['a0'][d]), wts['a_up_pad'][d],
             _row(wts['k_k']), _row(wts['k_a'])]
    if post:
        small += [wts['g_up'], _row(wts['r_k']), _row(wts['ln_w']), _row(wts['ln_b'])]
    for wgt in small:
        ins.append(wgt)
        specs.append(pl.BlockSpec(wgt.shape, const))
    out_shape = [jax.ShapeDtypeStruct((n, D_A), BF16 if post else F32)]
    out_specs = [pl.BlockSpec((BLOCK_T, D_A), blk)]
    if want_final:
        out_shape.append(jax.ShapeDtypeStruct((batch, D_A // HEAD, HEAD, HEAD), F32))
        out_specs.append(state_spec)
    n_items = (BLOCK_T // CHUNK) * (D_A // PAIR)
    scratch = [pltpu.VMEM((D_A // PAIR, PAIR, PAIR), F32)] + [pltpu.VMEM((BLOCK_T, D_A), F32)] * 4
    scratch += [pltpu.VMEM((n_items, CHUNK, PAIR), BF16), pltpu.VMEM((n_items, PAIR, PAIR), BF16),
                pltpu.VMEM((n_items, PAIR, PAIR), F32), pltpu.VMEM((n_items, CHUNK, PAIR), F32),
                pltpu.VMEM((8 * BLOCK_T // CHUNK, D_A), F32)]
    if post:
        scratch.append(pltpu.VMEM((BLOCK_T, D_A), F32))
    res = pl.pallas_call(
        functools.partial(_rwkv_kernel, rev=rev, has_init=has_init, want_final=want_final, post=post,
                          n_blk=n_blk),
        grid=(batch, n_blk),
        in_specs=specs, out_specs=out_specs, out_shape=out_shape, scratch_shapes=scratch,
        compiler_params=pltpu.CompilerParams(vmem_limit_bytes=VMEM_LIMIT),
        name="rwkv_bwd" if rev else "rwkv_fwd",
    )(*ins)
    return res[0], (res[1] if want_final else None)


DIAG = 4
HGRN_LEVELS = (4, 8, 16, 32)


def _block_edges(b, rev):
    t = _iota2(b.shape, 0)
    e = b
    out = {}
    for kbit in (1, 2, 4, 8, 16):
        if rev:
            e = jnp.where((t & kbit) != 0, pltpu.roll(e, kbit, 0), e)
        else:
            e = jnp.where((t & kbit) == 0, pltpu.roll(e, CHUNK - kbit, 0), e)
        if 2 * kbit in HGRN_LEVELS:
            out[2 * kbit] = e
    return out


def _hgrn_kernel(*refs, rev, has_init, want_final, post, n_blk):
    refs = list(refs)
    pc_ref = refs.pop(0)
    of_ref = refs.pop(0) if post else None
    s0_ref = refs.pop(0) if has_init else None
    lb_ref = refs.pop(0)
    ng_ref = refs.pop(0) if post else None
    o_ref = refs.pop(0)
    sfin_ref = refs.pop(0) if want_final else None
    s_scr, q_scr, logf_scr, kf_scr, f_scr, qe_scr, ds_scr, o0_scr, dec_scr = (refs.pop(0) for _ in range(9))
    ob_scr = refs.pop(0) if post else None

    n_pairs = D_C // PAIR
    j = pl.program_id(1)
    ind = _head_sum_matrix()

    @pl.when(j == 0)
    def _():
        if has_init:
            _load_state_pairs(s0_ref, s_scr, n_pairs, transpose=True)
        else:
            s_scr[...] = jnp.zeros(s_scr.shape, F32)

    q_scr[...] = _silu(pc_ref[:, 0:D_C])
    f_col = 2 * D_C if rev else D_C
    lb = lb_ref[...]
    f = lb + (1.0 - lb) * _sigmoid(pc_ref[:, f_col:f_col + D_C])
    f_scr[...] = f
    logf_scr[...] = jnp.log(f)
    kf_scr[...] = 1.0 - f

    tri = _cumsum_matrix(rev)
    same_head = _pair_masks(rev)[0]
    t_row, s_lane = _iota2((CHUNK, PAIR), 0), _iota2((CHUNK, PAIR), 1) & (CHUNK - 1)
    level_mask = {}
    for h in HGRN_LEVELS:
        sh = h.bit_length() - 1
        step = -1 if rev else 1
        level_mask[h] = ((t_row >> sh) == (s_lane >> sh) + step) & ((t_row >> (sh + 1)) == (s_lane >> (sh + 1)))
    out_buf = ob_scr if post else o_ref

    n_chunks = BLOCK_T // CHUNK

    for c0 in range(0, n_chunks, PHASE_A_CHUNKS):
        items = []
        for c in range(c0, c0 + PHASE_A_CHUNKS):
            rows = slice(c * CHUNK, (c + 1) * CHUNK)
            b_all = _dot_left01(tri, logf_scr[rows, :])
            for p in range(n_pairs):
                ln = slice(p * PAIR, (p + 1) * PAIR)
                b = b_all[:, ln]
                items.append(dict(
                    idx=c * n_pairs + p, b=b, b_last=b[0:1, :] if rev else b[CHUNK - 1:CHUNK, :],
                    q=q_scr[rows, ln], kf=kf_scr[rows, ln], f=f_scr[rows, ln],
                    v=pc_ref[rows, 3 * D_C + p * PAIR:3 * D_C + (p + 1) * PAIR]))
        for it in items:
            b, q, kf = it['b'], it['q'], it['kf']
            edges = _block_edges(b, rev)
            score = jnp.zeros((CHUNK, PAIR), F32)
            for h in HGRN_LEVELS:
                own = edges[h]
                partner = pltpu.roll(own, (CHUNK - h) if rev else h, 0)
                kp = (kf * jnp.exp(jnp.minimum(own - b, 0.0))).astype(BF16)
                qp = q * jnp.exp(jnp.minimum(b - partner, 0.0))
                sc = _dot_nt(qp, _stack_heads(kp))
                score = jnp.where(level_mask[h], sc, score)
            it['score'] = score.astype(BF16)
        for it in items:
            i = it['idx']
            v16 = it['v'].astype(BF16)
            qe_scr[i] = (it['q'] * jnp.exp(it['b'])).astype(BF16)
            k_e = (it['kf'] * jnp.exp(it['b_last'] - it['b'])).astype(BF16)
            ds = lax.dot_general(v16, k_e, (((0,), (0,)), ((), ())), preferred_element_type=F32)
            ds_scr[i] = jnp.where(same_head, ds, 0.0)
            dec_scr[i * 8:i * 8 + 1, :] = jnp.exp(it['b_last'])
            it['o'] = jnp.dot(it.pop('score'), _stack_heads(v16), preferred_element_type=F32)
        for delta in range(DIAG):
            for it in items:
                q, kf, f, v = it['q'], it['kf'], it['f'], it['v']
                if delta == 0:
                    x, v_s = q * kf, v
                else:
                    roll = lambda z, d: z if d == 0 else pltpu.roll(z, (CHUNK - d) if rev else d, 0)
                    it['e'] = f if delta == 1 else it['e'] * roll(f, delta - 1)
                    x = q * it['e'] * roll(kf, delta)
                    ok = ((t_row & (DIAG - 1)) <= DIAG - 1 - delta) if rev else ((t_row & (DIAG - 1)) >= delta)
                    x = jnp.where(ok, x, 0.0)
                    v_s = roll(v, delta)
                it['o'] = it['o'] + _dot(x, ind) * v_s
        for it in items:
            o0_scr[it['idx']] = it['o']

    for c in range(n_chunks):
        cc = (n_chunks - 1 - c) if rev else c
        rows = slice(cc * CHUNK, (cc + 1) * CHUNK)
        for p in range(n_pairs):
            i = cc * n_pairs + p
            s_bd = s_scr[p]
            o = lax.dot_general(qe_scr[i], s_bd.astype(BF16), (((1,), (1,)), ((), ())),
                                preferred_element_type=F32)
            out_buf[rows, p * PAIR:(p + 1) * PAIR] = o + o0_scr[i]
            s_scr[p] = s_bd * dec_scr[i * 8:i * 8 + 1, :] + ds_scr[i]

    if post:
        o = of_ref[...] + ob_scr[...]
        ms = _head_sums(o * o, _head_sum_matrix(D_C)) * (1.0 / HEAD)
        o = o * lax.rsqrt(ms + RMS_EPS) * ng_ref[...] * _silu(pc_ref[:, 4 * D_C:5 * D_C])
        o_ref[...] = o.astype(o_ref.dtype)

    if want_final:
        @pl.when(j == n_blk - 1)
        def _():
            _store_state_pairs(sfin_ref, s_scr, n_pairs, transpose=True)


def _hgrn_dir(pc, o_fwd, s0, lb, norm_g, *, batch, seq_len, rev, want_final):
    n = pc.shape[0]
    n_blk = seq_len // BLOCK_T
    post = rev
    has_init = s0 is not None
    blk = _blk_map(n_blk, rev)
    const = lambda b, j: (0, 0)
    state_spec = pl.BlockSpec((None, D_C // HEAD, HEAD, HEAD), lambda b, j: (b, 0, 0, 0))
    ins, specs = [pc], [pl.BlockSpec((BLOCK_T, PC_COLS), blk)]
    if post:
        ins.append(o_fwd)
        specs.append(pl.BlockSpec((BLOCK_T, D_C), blk))
    if has_init:
        ins.append(s0)
        specs.append(state_spec)
    ins.append(_row(lb))
    specs.append(pl.BlockSpec((1, D_C), const))
    if post:
        ins.append(_row(norm_g))
        specs.append(pl.BlockSpec((1, D_C), const))
    out_shape = [jax.ShapeDtypeStruct((n, D_C), BF16 if post else F32)]
    out_specs = [pl.BlockSpec((BLOCK_T, D_C), blk)]
    if want_final:
        out_shape.append(jax.ShapeDtypeStruct((batch, D_C // HEAD, HEAD, HEAD), F32))
        out_specs.append(state_spec)
    n_items = (BLOCK_T // CHUNK) * (D_C // PAIR)
    scratch = [pltpu.VMEM((D_C // PAIR, PAIR, PAIR), F32)] + [pltpu.VMEM((BLOCK_T, D_C), F32)] * 4
    scratch += [pltpu.VMEM((n_items, CHUNK, PAIR), BF16), pltpu.VMEM((n_items, PAIR, PAIR), F32),
                pltpu.VMEM((n_items, CHUNK, PAIR), F32), pltpu.VMEM((8 * n_items, PAIR), F32)]
    if post:
        scratch.append(pltpu.VMEM((BLOCK_T, D_C), F32))
    res = pl.pallas_call(
        functools.partial(_hgrn_kernel, rev=rev, has_init=has_init, want_final=want_final, post=post,
                          n_blk=n_blk),
        grid=(batch, n_blk),
        in_specs=specs, out_specs=out_specs, out_shape=out_shape, scratch_shapes=scratch,
        compiler_params=pltpu.CompilerParams(vmem_limit_bytes=VMEM_LIMIT),
        name="hgrn_bwd" if rev else "hgrn_fwd",
    )(*ins)
    return res[0], (res[1] if want_final else None)


def _shift_rows(x, s, t_idx):
    n = x.shape[0]
    y = pltpu.roll(x, s % n, 0)
    return jnp.where((t_idx >= s) if s > 0 else (t_idx < n + s), y, 0.0)


SCAN_GROUP = 8


def _group_scan(a, b, t_idx, rev):
    n = a.shape[0]
    in_group = t_idx & (SCAN_GROUP - 1)
    k = 1
    while k < SCAN_GROUP:
        ok = (in_group < SCAN_GROUP - k) if rev else (in_group >= k)
        sh = (n - k) if rev else k
        b = jnp.where(ok, a * pltpu.roll(b, sh, 0) + b, b)
        a = jnp.where(ok, a * pltpu.roll(a, sh, 0), a)
        k *= 2
    return a, b


def _gelu_tanh(x):
    return 0.5 * x * (1.0 + jnp.tanh(math.sqrt(2.0 / math.pi) * (x + 0.044715 * x * x * x)))


def _lru_kernel(*refs, has_init, want_final):
    refs = list(refs)
    pb_ref = refs.pop(0)
    h0_ref = refs.pop(0) if has_init else None
    cw_ref, cb_ref, wa_ref, ba_ref, wx_ref, bx_ref, lam_ref = (refs.pop(0) for _ in range(7))
    y_ref = refs.pop(0)
    hfin_ref = refs.pop(0) if want_final else None
    a_scr, b_scr, h_scr = (refs.pop(0) for _ in range(3))

    n = pb_ref.shape[0]
    t_idx = _iota2((n, D_B), 0)
    xb = pb_ref[:, 0:D_B]
    u = cb_ref[...] + cw_ref[2:3, :] * xb
    u = u + cw_ref[0:1, :] * _shift_rows(xb, 2, t_idx)
    u = u + cw_ref[1:2, :] * _shift_rows(xb, 1, t_idx)
    u = u + cw_ref[3:4, :] * _shift_rows(xb, -1, t_idx)
    for d in range(2):
        rg = _sigmoid(_dot(u, wa_ref[d]) + ba_ref[d:d + 1, :])
        ig = _sigmoid(_dot(u, wx_ref[d]) + bx_ref[d:d + 1, :])
        log_a = -RG_C * rg * _softplus(-lam_ref[d:d + 1, :])
        a_t = jnp.exp(log_a)
        in_scale = jnp.sqrt((1.0 + a_t * a_t) * jnp.tanh(-log_a))
        a_scr[d], b_scr[d] = _group_scan(a_t, in_scale * (ig * u), t_idx, rev=d == 1)

    n_groups = n // SCAN_GROUP

    def carry_body(g, carry):
        h_f, h_b = carry
        rows_f = pl.ds(pl.multiple_of(g * SCAN_GROUP, SCAN_GROUP), SCAN_GROUP)
        rows_b = pl.ds(pl.multiple_of((n_groups - 1 - g) * SCAN_GROUP, SCAN_GROUP), SCAN_GROUP)
        blk_f = b_scr[0, rows_f, :] + a_scr[0, rows_f, :] * h_f
        blk_b = b_scr[1, rows_b, :] + a_scr[1, rows_b, :] * h_b
        h_scr[0, rows_f, :] = blk_f
        h_scr[1, rows_b, :] = blk_b
        return blk_f[SCAN_GROUP - 1:SCAN_GROUP, :], blk_b[0:1, :]

    if has_init:
        init = (h0_ref[0:1, :], h0_ref[1:2, :])
    else:
        init = (jnp.zeros((1, D_B), F32), jnp.zeros((1, D_B), F32))
    h_f, h_b = lax.fori_loop(0, n_groups, carry_body, init, unroll=4)
    if want_final:
        hfin_ref[0:1, :] = h_f
        hfin_ref[1:2, :] = h_b
    y_ref[...] = ((h_scr[0] + h_scr[1]) * _gelu_tanh(pb_ref[:, D_B:2 * D_B])).astype(y_ref.dtype)


def _lru(pb, h0, wts, *, batch, seq_len, want_final):
    n = pb.shape[0]
    has_init = h0 is not None
    const2 = lambda b: (0, 0)
    const3 = lambda b: (0, 0, 0)
    ins, specs = [pb], [pl.BlockSpec((seq_len, PB_COLS), lambda b: (b, 0))]
    state_spec = pl.BlockSpec((None, 2, D_B), lambda b: (b, 0, 0))
    if has_init:
        ins.append(h0)
        specs.append(state_spec)
    for wgt in (wts['conv_w'], _row(wts['conv_b']), wts['wa_bd'], wts['ba'], wts['wx_bd'], wts['bx'],
                wts['lam']):
        ins.append(wgt)
        specs.append(pl.BlockSpec(wgt.shape, const3 if wgt.ndim == 3 else const2))
    out_shape = [jax.ShapeDtypeStruct((n, D_B), BF16)]
    out_specs = [pl.BlockSpec((seq_len, D_B), lambda b: (b, 0))]
    if want_final:
        out_shape.append(jax.ShapeDtypeStruct((batch, 2, D_B), F32))
        out_specs.append(state_spec)
    res = pl.pallas_call(
        functools.partial(_lru_kernel, has_init=has_init, want_final=want_final),
        grid=(batch,),
        in_specs=specs, out_specs=out_specs, out_shape=out_shape,
        scratch_shapes=[pltpu.VMEM((2, seq_len, D_B), F32)] * 3,
        compiler_params=pltpu.CompilerParams(vmem_limit_bytes=VMEM_LIMIT),
        name="lru",
    )(*ins)
    return res[0], (res[1] if want_final else None)


def _out_ffn_kernel(*refs, conv_period, final):
    refs = list(refs)
    x_ref, ma_ref, mb_ref, mc_ref, mod_ref, g_ref, wo_ref, wg_ref, wu_ref, wd_ref, cw_ref, cb_ref = (
        refs.pop(0) for _ in range(12))
    fg_ref = refs.pop(0) if final else None
    xo_ref = refs.pop(0)
    yo_ref = refs.pop(0) if final else None

    dm = D_MODEL
    mix = jnp.dot(ma_ref[...], wo_ref[0:D_A, :], preferred_element_type=F32)
    mix = mix + jnp.dot(mb_ref[...], wo_ref[D_A:D_A + D_B, :], preferred_element_type=F32)
    mix = mix + jnp.dot(mc_ref[...], wo_ref[D_A + D_B:, :], preferred_element_type=F32)
    x = x_ref[...] + mod_ref[:, 2 * dm:3 * dm] * mix
    h = _rms(x) * g_ref[...]
    h = (h * (1.0 + mod_ref[:, 4 * dm:5 * dm]) + mod_ref[:, 3 * dm:4 * dm]).astype(BF16)
    tm = x.shape[0]
    t_idx = _iota2((tm, FF_TILE), 0) & (conv_period - 1)
    acc = jnp.zeros((tm, dm), F32)
    for f0 in range(0, D_FF, FF_TILE):
        cols = slice(f0, f0 + FF_TILE)
        g = jnp.dot(h, wg_ref[:, cols], preferred_element_type=F32)
        g_prev = jnp.where(t_idx == 0, 0.0, pltpu.roll(g, 1, 0))
        g_next = jnp.where(t_idx == conv_period - 1, 0.0, pltpu.roll(g, tm - 1, 0))
        g = cw_ref[0:1, cols] * g_prev + cw_ref[1:2, cols] * g + cw_ref[2:3, cols] * g_next + cb_ref[:, cols]
        up = jnp.dot(h, wu_ref[:, cols], preferred_element_type=F32)
        acc = acc + jnp.dot((_silu(g) * up).astype(BF16), wd_ref[cols, :], preferred_element_type=F32)
    x = x + mod_ref[:, 5 * dm:6 * dm] * acc
    xo_ref[...] = x
    if final:
        yo_ref[...] = _rms(x) * fg_ref[...]


def _out_ffn(x, mix_a, mix_b, mix_c, mods, wts, *, seq_len, conv_period, final_g):
    n = x.shape[0]
    final = final_g is not None
    tiles_per_seq = seq_len // TOKEN_TILE if mods.shape[0] > 1 else None
    mod_map = (lambda i: (i // tiles_per_seq, 0, 0)) if tiles_per_seq else (lambda i: (0, 0, 0))
    const = lambda i: (0, 0)
    tok = lambda width: pl.BlockSpec((TOKEN_TILE, width), lambda i: (i, 0))
    resident = lambda w: _layer_resident(w, wts['layer'])
    ins = [x, mix_a, mix_b, mix_c, mods, _row(wts['norm_g']), wts['w_out'], wts['w_gate'], wts['w_up'],
           wts['w_down'], wts['conv_w'], _row(wts['conv_b'])]
    specs = [tok(D_MODEL), tok(D_A), tok(D_B), tok(D_C), pl.BlockSpec((None, 1, N_MOD * D_MODEL), mod_map),
             pl.BlockSpec((1, D_MODEL), const), resident(wts['w_out']), resident(wts['w_gate']),
             resident(wts['w_up']), resident(wts['w_down']), pl.BlockSpec(wts['conv_w'].shape, const),
             pl.BlockSpec((1, D_FF), const)]
    out_shape = [jax.ShapeDtypeStruct((n, D_MODEL), F32)]
    out_specs = [tok(D_MODEL)]
    if final:
        ins.append(_row(final_g))
        specs.append(pl.BlockSpec((1, D_MODEL), const))
        out_shape.append(jax.ShapeDtypeStruct((n, D_MODEL), F32))
        out_specs.append(tok(D_MODEL))
    res = pl.pallas_call(
        functools.partial(_out_ffn_kernel, conv_period=conv_period, final=final),
        grid=(n // TOKEN_TILE,),
        in_specs=specs, out_specs=out_specs, out_shape=out_shape,
        compiler_params=pltpu.CompilerParams(vmem_limit_bytes=VMEM_LIMIT),
        name="out_ffn",
    )(*ins)
    return res[0], (res[1] if final else None)


def _trunk_layer(x, mods, s_rwkv, s_lru, s_hgrn, wts, *, batch, seq_len, conv_period, want_final, final_g):
    pa, pb, pc = _norm_in(x, mods, _row(wts['norm_mix_g']), wts['w_in'], wts['layer'], seq_len)
    kw = dict(batch=batch, seq_len=seq_len, want_final=want_final)
    sel = lambda s, d: None if s is None else s[:, d]
    o_f, sa_f = _rwkv_dir(pa, None, sel(s_rwkv, 0), wts['rwkv'], rev=False, **kw)
    mix_a, sa_b = _rwkv_dir(pa, o_f, sel(s_rwkv, 1), wts['rwkv'], rev=True, **kw)
    c_f, sc_f = _hgrn_dir(pc, None, sel(s_hgrn, 0), wts['hgrn_lb'][0], wts['hgrn_norm_g'], rev=False, **kw)
    mix_c, sc_b = _hgrn_dir(pc, c_f, sel(s_hgrn, 1), wts['hgrn_lb'][1], wts['hgrn_norm_g'], rev=True, **kw)
    mix_b, sb = _lru(pb, s_lru, wts['lru'], **kw)
    x, y = _out_ffn(x, mix_a, mix_b, mix_c, mods, wts['ffn'], seq_len=seq_len, conv_period=conv_period,
                    final_g=final_g)
    states = None
    if want_final:
        states = ([sa_f, sa_b], sb, [sc_f, sc_b])
    return x, y, states


def _block_diag(w):
    nd, nb, bi, bj = w.shape
    eye = jnp.eye(nb, dtype=w.dtype)
    return jnp.einsum('dhij,hg->dhigj', w, eye).reshape(nd, nb * bi, nb * bj)


def _pad_dir_rows(w):
    z = jnp.zeros_like(w[0])
    return jnp.stack([jnp.concatenate([w[0], z], axis=0), jnp.concatenate([z, w[1]], axis=0)], axis=0)


def kernel(x_prompt, x_sample, state_rwkv, state_rglru, state_hgrn, c, c_ctx, ada_w, ada_b, norm_mix_g, norm_ffn_g, w_in, w_out, rwkv_w0, rwkv_w_up, rwkv_a0, rwkv_a_up, rwkv_g_up, rwkv_k_k, rwkv_k_a, rwkv_r_k, rwkv_ln_w, rwkv_ln_b, lru_conv_w, lru_conv_b, lru_wa, lru_ba, lru_wx, lru_bx, lru_lambda, hgrn_lb_logits, hgrn_norm_g, ffn_w_gate, ffn_w_up, ffn_conv_w, ffn_conv_b, ffn_w_down, final_g):
    n_ctx, t_ctx, dm = x_prompt.shape
    n_lat, t_lat, _ = x_sample.shape
    depth = w_in.shape[0]
    grid_w = 64

    cond = jnp.concatenate([c_ctx[None, :], c, jnp.zeros((8 - 1 - n_lat, dm), F32)], axis=0)
    mods = _modulation(cond, ada_w, ada_b.reshape(depth, 1, -1))
    lb_all = jnp.cumsum(jax.nn.softmax(hgrn_lb_logits.astype(F32), axis=1), axis=1)
    lb_all = lb_all - lb_all[:, :1]

    xp = x_prompt.reshape(n_ctx * t_ctx, dm)
    xs = x_sample.reshape(n_lat * t_lat, dm)
    new_rwkv, new_lru, new_hgrn = [], [], []
    yp = ys = None
    w_in16, w_out16 = _to_bf16(w_in), _to_bf16(w_out)
    w_gate16, w_up16, w_down16 = _to_bf16(ffn_w_gate), _to_bf16(ffn_w_up), _to_bf16(ffn_w_down)
    for l in range(depth):
        wts = {
            'norm_mix_g': norm_mix_g[l], 'layer': l,
            'w_in': w_in16,
            'rwkv': {'w0': rwkv_w0[l], 'w_up_pad': _pad_dir_rows(rwkv_w_up[l]).astype(BF16),
                     'a0': rwkv_a0[l], 'a_up_pad': _pad_dir_rows(rwkv_a_up[l]).astype(BF16),
                     'g_up': rwkv_g_up[l].astype(BF16), 'k_k': rwkv_k_k[l], 'k_a': rwkv_k_a[l],
                     'r_k': rwkv_r_k[l], 'ln_w': rwkv_ln_w[l], 'ln_b': rwkv_ln_b[l]},
            'lru': {'conv_w': lru_conv_w[l], 'conv_b': lru_conv_b[l],
                    'wa_bd': _block_diag(lru_wa[l]).astype(BF16), 'ba': lru_ba[l],
                    'wx_bd': _block_diag(lru_wx[l]).astype(BF16), 'bx': lru_bx[l], 'lam': lru_lambda[l]},
            'hgrn_lb': lb_all[:, l], 'hgrn_norm_g': hgrn_norm_g[l],
            'ffn': {'norm_g': norm_ffn_g[l], 'layer': l, 'w_out': w_out16, 'w_gate': w_gate16, 'w_up': w_up16,
                    'w_down': w_down16, 'conv_w': ffn_conv_w[l], 'conv_b': ffn_conv_b[l]},
        }
        last = l == depth - 1
        fg = final_g if last else None
        xp, yp, st = _trunk_layer(xp, mods[l, 0:1].reshape(1, 1, -1), None, None, None, wts,
                                  batch=n_ctx, seq_len=t_ctx, conv_period=t_ctx, want_final=True, final_g=fg)
        new_rwkv += st[0]
        new_lru.append(st[1])
        new_hgrn += st[2]
        xs, ys, _ = _trunk_layer(xs, mods[l, 1:1 + n_lat].reshape(n_lat, 1, -1),
                                 state_rwkv[:, l].astype(F32), state_rglru[:, l].astype(F32),
                                 state_hgrn[:, l].astype(F32), wts,
                                 batch=n_lat, seq_len=t_lat, conv_period=grid_w, want_final=False, final_g=fg)
    return (yp.reshape(n_ctx, t_ctx, dm), ys.reshape(n_lat, t_lat, dm),
            jnp.stack(new_rwkv, axis=1).reshape((n_ctx, depth, 2) + new_rwkv[0].shape[1:]),
            jnp.stack(new_lru, axis=1),
            jnp.stack(new_hgrn, axis=1).reshape((n_ctx, depth, 2) + new_hgrn[0].shape[1:]))
```
